```python
import math
import jax, jax.numpy as jnp
from jax import lax
import numpy as np

D_MODEL = 4096
BATCH = 4
SEQ = 2048
DEPTH = 2
DEC_BATCH = 128
DEC_SEQ = 4
PAST_LEN = 16384
PAGE_SIZE = 128

D_MIX = D_MODEL
D_SSM = D_MIX // 2
D_GDN = D_MIX - D_SSM
SSM_HEAD_DIM = 64
SSM_HEADS = D_SSM // SSM_HEAD_DIM
SSM_GROUPS = 4
SSM_HPG = SSM_HEADS // SSM_GROUPS
D_STATE = 128
SSM_GN = SSM_GROUPS * D_STATE
SSM_CONV_DIM = D_SSM + 2 * SSM_GN
GDN_HEAD_DIM = 128
GDN_HEADS = D_GDN // GDN_HEAD_DIM
GDN_CONV_DIM = 3 * D_GDN
CONV_K = 4
CHUNK = 128
SPLIT_Z = D_SSM
SPLIT_XBC = SPLIT_Z + SSM_CONV_DIM
SPLIT_DT = SPLIT_XBC + SSM_HEADS
SPLIT_QKV = SPLIT_DT + GDN_CONV_DIM
SPLIT_GATE = SPLIT_QKV + D_GDN
SPLIT_BETA = SPLIT_GATE + GDN_HEADS
IN_COLS = SPLIT_BETA + GDN_HEADS
SPLIT_POINTS = (SPLIT_Z, SPLIT_XBC, SPLIT_DT, SPLIT_QKV, SPLIT_GATE, SPLIT_BETA)
D_FF = 14336
N_EXPERTS = 8
TOP_K = 2
D_FF_EXPERT = 14336
N_DENSE = (DEPTH + 1) // 2
N_MOE = DEPTH // 2
EPS = 1e-6

kernel_name = 'hymba_ssd_gdn_moe_decode_step'


def rms(x):
    xf = x.astype(jnp.float32)
    return xf * lax.rsqrt(jnp.mean(xf * xf, axis=-1, keepdims=True) + EPS)


def rms_norm(x, gain):
    return (rms(x) * gain.astype(jnp.float32)).astype(x.dtype)


def l2norm(x):
    return x * lax.rsqrt(jnp.sum(x * x, axis=-1, keepdims=True) + EPS)


def short_conv(x, buf, w):
    seq = x.shape[1]
    xp = jnp.concatenate([buf.astype(x.dtype), x], axis=1)
    wf = w.astype(x.dtype)
    y = wf[0] * xp[:, 0:seq]
    for j in range(1, CONV_K):
        y = y + wf[j] * xp[:, j:j + seq]
    return y, xp[:, seq:]


def ssd_chunked(x, dt, a_log, b_in, c_in, h0):
    bsz, seq = x.shape[0], x.shape[1]
    q = min(CHUNK, seq)
    nc = seq // q
    a = dt * (-jnp.exp(a_log))

    def to_chunks(t):
        return jnp.moveaxis(t.reshape((bsz, nc, q) + t.shape[2:]), 1, 0)

    xg = x.reshape(bsz, seq, SSM_GROUPS, SSM_HPG, SSM_HEAD_DIM)
    dtg = dt.reshape(bsz, seq, SSM_GROUPS, SSM_HPG)
    ag = a.reshape(bsz, seq, SSM_GROUPS, SSM_HPG)
    causal = jnp.tril(jnp.ones((q, q), dtype=bool))[None, :, :, None, None]

    def body(h, inp):
        xc, dtc, ac, bc, cc = inp
        acs = jnp.cumsum(ac, axis=1)
        seg = acs[:, :, None] - acs[:, None]
        decay = jnp.exp(jnp.where(causal, seg, -jnp.inf))
        cb = jnp.einsum('btgn,bsgn->btsg', cc, bc)
        w = cb[..., None] * decay * dtc[:, None]
        y = jnp.einsum('btsgr,bsgrp->btgrp', w, xc)
        y = y + jnp.exp(acs)[..., None] * jnp.einsum('btgn,bgrpn->btgrp', cc, h)
        last = acs[:, -1]
        wts = jnp.exp(last[:, None] - acs) * dtc
        h = jnp.exp(last)[..., None, None] * h + jnp.einsum('bsgr,bsgn,bsgrp->bgrpn', wts, bc, xc)
        return h, y

    h0g = h0.reshape(bsz, SSM_GROUPS, SSM_HPG, SSM_HEAD_DIM, D_STATE)
    h_last, ys = lax.scan(body, h0g, (to_chunks(xg), to_chunks(dtg), to_chunks(ag),
                                      to_chunks(b_in), to_chunks(c_in)))
    y = jnp.moveaxis(ys, 0, 1).reshape(bsz, seq, SSM_HEADS, SSM_HEAD_DIM)
    return y, h_last.reshape(bsz, SSM_HEADS, SSM_HEAD_DIM, D_STATE)


def gdn_chunked(q, k, v, g, beta, s0):
    bsz, seq = q.shape[0], q.shape[1]
    c = min(CHUNK, seq)
    nc = seq // c

    def to_chunks(t):
        t = jnp.moveaxis(t.reshape((bsz, nc, c) + t.shape[2:]), 1, 0)
        return jnp.moveaxis(t, 3, 2)

    qc, kc, vc, gc, bc = to_chunks(q), to_chunks(k), to_chunks(v), to_chunks(g), to_chunks(beta)
    gcs = jnp.cumsum(gc, axis=-1)
    seg = gcs[..., :, None] - gcs[..., None, :]
    incl = jnp.tril(jnp.ones((c, c), dtype=bool))
    strict = jnp.tril(jnp.ones((c, c), dtype=bool), -1)
    decay = jnp.exp(jnp.where(incl, seg, -jnp.inf))
    kk = jnp.einsum('nbhtk,nbhsk->nbhts', kc, kc)
    m = jnp.eye(c, dtype=q.dtype) + jnp.where(strict, bc[..., :, None] * kk * decay, 0.0)
    rhs = jnp.concatenate([vc * bc[..., None], kc * (bc * jnp.exp(gcs))[..., None]], axis=-1)
    sol = lax.linalg.triangular_solve(m, rhs, left_side=True, lower=True, unit_diagonal=True)
    u, w = sol[..., :GDN_HEAD_DIM], sol[..., GDN_HEAD_DIM:]
    qk = jnp.einsum('nbhtk,nbhsk->nbhts', qc, kc) * decay
    q_dec = qc * jnp.exp(gcs)[..., None]
    k_dec = kc * jnp.exp(gcs[..., -1:] - gcs)[..., None]
    chunk_decay = jnp.exp(gcs[..., -1])

    def body(s, inp):
        u_c, w_c, qd_c, kd_c, qk_c, cd_c = inp
        v_new = u_c - jnp.einsum('bhtk,bhkv->bhtv', w_c, s)
        o = jnp.einsum('bhtk,bhkv->bhtv', qd_c, s) + jnp.einsum('bhts,bhsv->bhtv', qk_c, v_new)
        s = cd_c[..., None, None] * s + jnp.einsum('bhsk,bhsv->bhkv', kd_c, v_new)
        return s, o

    s_last, os_ = lax.scan(body, s0, (u, w, q_dec, k_dec, qk, chunk_decay))
    o = jnp.swapaxes(jnp.moveaxis(os_, 0, 1), 2, 3).reshape(bsz, seq, GDN_HEADS, GDN_HEAD_DIM)
    return o, s_last


def token_mixers(h, ssm_h, ssm_buf, gdn_s, gdn_buf, w_in, ssm_conv_w, ssm_conv_b, ssm_dt_bias,
                 ssm_a_log, ssm_d, ssm_norm, gdn_conv_w, gdn_dt_bias, gdn_a_log, gdn_norm, w_out):
    f32 = jnp.float32
    bsz, seq = h.shape[0], h.shape[1]
    proj = jnp.matmul(h, w_in).astype(f32)
    z, xbc, dt_raw, qkv, gate, b_raw, a_raw = jnp.split(proj, SPLIT_POINTS, axis=-1)
    xbc, ssm_buf_new = short_conv(xbc, ssm_buf, ssm_conv_w)
    xbc = jax.nn.silu(xbc + ssm_conv_b.astype(f32))
    xs = xbc[..., :D_SSM].reshape(bsz, seq, SSM_HEADS, SSM_HEAD_DIM)
    bs = xbc[..., D_SSM:D_SSM + SSM_GN].reshape(bsz, seq, SSM_GROUPS, D_STATE)
    cs = xbc[..., D_SSM + SSM_GN:].reshape(bsz, seq, SSM_GROUPS, D_STATE)
    dt = jax.nn.softplus(dt_raw + ssm_dt_bias.astype(f32))
    y, ssm_h_new = ssd_chunked(xs, dt, ssm_a_log.astype(f32), bs, cs, ssm_h.astype(f32))
    y = y + ssm_d.astype(f32)[:, None] * xs
    y = y.reshape(bsz, seq, D_SSM) * jax.nn.silu(z)
    y = rms(y.reshape(bsz, seq, SSM_GROUPS, D_SSM // SSM_GROUPS)).reshape(bsz, seq, D_SSM) * ssm_norm.astype(f32)
    qkv, gdn_buf_new = short_conv(qkv, gdn_buf, gdn_conv_w)
    qkv = jax.nn.silu(qkv)
    q, k, v = jnp.split(qkv, 3, axis=-1)
    q = l2norm(q.reshape(bsz, seq, GDN_HEADS, GDN_HEAD_DIM)) * (GDN_HEAD_DIM ** -0.5)
    k = l2norm(k.reshape(bsz, seq, GDN_HEADS, GDN_HEAD_DIM))
    v = v.reshape(bsz, seq, GDN_HEADS, GDN_HEAD_DIM)
    beta = jax.nn.sigmoid(b_raw)
    g = -jnp.exp(gdn_a_log.astype(f32)) * jax.nn.softplus(a_raw + gdn_dt_bias.astype(f32))
    o, gdn_s_new = gdn_chunked(q, k, v, g, beta, gdn_s.astype(f32))
    o = rms(o) * gdn_norm.astype(f32) * jax.nn.silu(gate.reshape(bsz, seq, GDN_HEADS, GDN_HEAD_DIM))
    mixed = jnp.concatenate([y, o.reshape(bsz, seq, D_GDN)], axis=-1).astype(h.dtype)
    out = jnp.matmul(mixed, w_out)
    return out, ssm_h_new, ssm_buf_new, gdn_s_new, gdn_buf_new


def swiglu(h, w_gate, w_up, w_down):
    return jnp.matmul(jax.nn.silu(jnp.matmul(h, w_gate)) * jnp.matmul(h, w_up), w_down)


def moe_swiglu(h, router, w_gate, w_up, w_down):
    logits = jnp.matmul(h, router).astype(jnp.float32)
    top_v, top_i = lax.top_k(logits, TOP_K)
    probs = jax.nn.softmax(top_v, axis=-1)
    gates = jnp.sum(jax.nn.one_hot(top_i, N_EXPERTS, dtype=jnp.float32) * probs[..., None], axis=-2).astype(h.dtype)
    out = jnp.zeros_like(h)
    for e in range(N_EXPERTS):
        out = out + gates[..., e:e + 1] * swiglu(h, w_gate[e], w_up[e], w_down[e])
    return out


def trunk(x, st_ssm, st_ssm_conv, st_gdn, st_gdn_conv, prm):
    new_ssm, new_ssm_conv, new_gdn, new_gdn_conv = [], [], [], []
    for i in range(DEPTH):
        h = rms_norm(x, prm['norm_mix'][i])
        mix, s1, c1, s2, c2 = token_mixers(
            h, st_ssm[i], st_ssm_conv[i], st_gdn[i], st_gdn_conv[i], prm['w_in'][i],
            prm['ssm_conv_w'][i], prm['ssm_conv_b'][i], prm['ssm_dt_bias'][i], prm['ssm_a_log'][i],
            prm['ssm_d'][i], prm['ssm_norm'][i], prm['gdn_conv_w'][i], prm['gdn_dt_bias'][i],
            prm['gdn_a_log'][i], prm['gdn_norm'][i], prm['w_out'][i])
        x = x + mix
        h = rms_norm(x, prm['norm_ffn'][i])
        j = i // 2
        if i % 2 == 0:
            ffn = swiglu(h, prm['dense_w_gate'][j], prm['dense_w_up'][j], prm['dense_w_down'][j])
        else:
            ffn = moe_swiglu(h, prm['moe_router'][j], prm['moe_w_gate'][j], prm['moe_w_up'][j], prm['moe_w_down'][j])
        x = x + ffn
        new_ssm.append(s1.astype(st_ssm.dtype))
        new_ssm_conv.append(c1.astype(st_ssm_conv.dtype))
        new_gdn.append(s2.astype(st_gdn.dtype))
        new_gdn_conv.append(c2.astype(st_gdn_conv.dtype))
    y = rms_norm(x, prm['norm_final'])
    return y, jnp.stack(new_ssm), jnp.stack(new_ssm_conv), jnp.stack(new_gdn), jnp.stack(new_gdn_conv)


def setup_inputs(seed: int = 0) -> dict:
    key = jax.random.key(seed)
    ks = jax.random.split(key, 28)
    f32 = jnp.float32
    D, L = D_MODEL, DEPTH

    def nrm(k, shape, scale):
        return jax.random.normal(k, shape, f32) * scale

    def gain(k, shape):
        return 1.0 + 0.01 * jax.random.normal(k, shape, f32)

    def dt_bias(k, shape):
        dt = jnp.exp(jax.random.uniform(k, shape, f32, math.log(1e-3), math.log(1e-1)))
        return dt + jnp.log(-jnp.expm1(-dt))

    def a_log(k, shape):
        return jnp.log(jax.random.uniform(k, shape, f32, 1.0, 16.0))

    return {
        'x_prompt': nrm(ks[0], (BATCH, SEQ, D), 1.0),
        'x_sample': nrm(ks[1], (DEC_BATCH, DEC_SEQ, D), 1.0),
        'state_ssm': nrm(ks[2], (L, DEC_BATCH, SSM_HEADS, SSM_HEAD_DIM, D_STATE), 0.1),
        'state_ssm_conv': nrm(ks[3], (L, DEC_BATCH, CONV_K - 1, SSM_CONV_DIM), 1.0),
        'state_gdn': nrm(ks[4], (L, DEC_BATCH, GDN_HEADS, GDN_HEAD_DIM, GDN_HEAD_DIM), 0.1),
        'state_gdn_conv': nrm(ks[5], (L, DEC_BATCH, CONV_K - 1, GDN_CONV_DIM), 1.0),
        'norm_mix': gain(ks[6], (L, D)),
        'w_in': nrm(ks[7], (L, D, IN_COLS), D ** -0.5),
        'ssm_conv_w': nrm(ks[8], (L, CONV_K, SSM_CONV_DIM), CONV_K ** -0.5),
        'ssm_conv_b': nrm(ks[9], (L, SSM_CONV_DIM), 0.01),
        'ssm_dt_bias': dt_bias(ks[10], (L, SSM_HEADS)),
        'ssm_a_log': a_log(ks[11], (L, SSM_HEADS)),
        'ssm_d': gain(ks[12], (L, SSM_HEADS)),
        'ssm_norm': gain(ks[13], (L, D_SSM)),
        'gdn_conv_w': nrm(ks[14], (L, CONV_K, GDN_CONV_DIM), CONV_K ** -0.5),
        'gdn_dt_bias': dt_bias(ks[15], (L, GDN_HEADS)),
        'gdn_a_log': a_log(ks[16], (L, GDN_HEADS)),
        'gdn_norm': gain(ks[17], (L, GDN_HEAD_DIM)),
        'w_out': nrm(ks[18], (L, D_MIX, D), D_MIX ** -0.5),
        'norm_ffn': gain(ks[19], (L, D)),
        'dense_w_gate': nrm(ks[20], (N_DENSE, D, D_FF), D ** -0.5),
        'dense_w_up': nrm(ks[21], (N_DENSE, D, D_FF), D ** -0.5),
        'dense_w_down': nrm(ks[22], (N_DENSE, D_FF, D), D_FF ** -0.5),
        'moe_router': nrm(ks[23], (N_MOE, D, N_EXPERTS), D ** -0.5),
        'moe_w_gate': nrm(ks[24], (N_MOE, N_EXPERTS, D, D_FF_EXPERT), D ** -0.5),
        'moe_w_up': nrm(ks[25], (N_MOE, N_EXPERTS, D, D_FF_EXPERT), D ** -0.5),
        'moe_w_down': nrm(ks[26], (N_MOE, N_EXPERTS, D_FF_EXPERT, D), D_FF_EXPERT ** -0.5),
        'norm_final': gain(ks[27], (D,)),
    }


def reference(x_prompt, x_sample, state_ssm, state_ssm_conv, state_gdn, state_gdn_conv,
              norm_mix, w_in, ssm_conv_w, ssm_conv_b, ssm_dt_bias, ssm_a_log, ssm_d, ssm_norm,
              gdn_conv_w, gdn_dt_bias, gdn_a_log, gdn_norm, w_out, norm_ffn,
              dense_w_gate, dense_w_up, dense_w_down, moe_router, moe_w_gate, moe_w_up, moe_w_down,
              norm_final):
    prm = {
        'norm_mix': norm_mix, 'w_in': w_in, 'ssm_conv_w': ssm_conv_w, 'ssm_conv_b': ssm_conv_b,
        'ssm_dt_bias': ssm_dt_bias, 'ssm_a_log': ssm_a_log, 'ssm_d': ssm_d, 'ssm_norm': ssm_norm,
        'gdn_conv_w': gdn_conv_w, 'gdn_dt_bias': gdn_dt_bias, 'gdn_a_log': gdn_a_log, 'gdn_norm': gdn_norm,
        'w_out': w_out, 'norm_ffn': norm_ffn, 'dense_w_gate': dense_w_gate, 'dense_w_up': dense_w_up,
        'dense_w_down': dense_w_down, 'moe_router': moe_router, 'moe_w_gate': moe_w_gate,
        'moe_w_up': moe_w_up, 'moe_w_down': moe_w_down, 'norm_final': norm_final,
    }
    bp = x_prompt.shape[0]
    z_ssm = jnp.zeros((DEPTH, bp, SSM_HEADS, SSM_HEAD_DIM, D_STATE), state_ssm.dtype)
    z_ssm_conv = jnp.zeros((DEPTH, bp, CONV_K - 1, SSM_CONV_DIM), state_ssm_conv.dtype)
    z_gdn = jnp.zeros((DEPTH, bp, GDN_HEADS, GDN_HEAD_DIM, GDN_HEAD_DIM), state_gdn.dtype)
    z_gdn_conv = jnp.zeros((DEPTH, bp, CONV_K - 1, GDN_CONV_DIM), state_gdn_conv.dtype)
    y_prompt, nsp, nscp, ngp, ngcp = trunk(x_prompt, z_ssm, z_ssm_conv, z_gdn, z_gdn_conv, prm)
    y_sample, nss, nscs, ngs, ngcs = trunk(x_sample, state_ssm, state_ssm_conv, state_gdn, state_gdn_conv, prm)
    return (y_prompt, y_sample, nsp, nscp, ngp, ngcp, nss, nscs, ngs, ngcs)
```

```python
import functools

import jax
import jax.numpy as jnp
from jax import lax
from jax.experimental import pallas as pl
from jax.experimental.pallas import tpu as pltpu

F32 = jnp.float32
BF16 = jnp.bfloat16

D_MODEL = 4096
DEPTH = 2
D_SSM = 2048
D_GDN = 2048
SSM_HEAD_DIM = 64
SSM_HEADS = 32
SSM_GROUPS = 4
SSM_HPG = 8
D_STATE = 128
SSM_GN = SSM_GROUPS * D_STATE
SSM_CONV_DIM = D_SSM + 2 * SSM_GN
GDN_HEAD_DIM = 128
GDN_HEADS = 16
GDN_CONV_DIM = 3 * D_GDN
CONV_K = 4
CHUNK = 128
SPLIT_Z = D_SSM
SPLIT_XBC = SPLIT_Z + SSM_CONV_DIM
SPLIT_DT = SPLIT_XBC + SSM_HEADS
SPLIT_QKV = SPLIT_DT + GDN_CONV_DIM
SPLIT_GATE = SPLIT_QKV + D_GDN
SPLIT_BETA = SPLIT_GATE + GDN_HEADS
IN_COLS = SPLIT_BETA + GDN_HEADS
D_FF = 14336
N_EXPERTS = 8
TOP_K = 2
EPS = 1e-6

LANES = 128
T_PAD = 128
DEC_ROWS = 16
VMEM_LIMIT = 56 * 1024 * 1024


def _cparams(sem):
    return pltpu.CompilerParams(dimension_semantics=sem, vmem_limit_bytes=VMEM_LIMIT)


def _dot(a, b):
    return jnp.dot(a.astype(BF16), b.astype(BF16), preferred_element_type=F32)


def _dot_nt(a, b):
    return lax.dot_general(a.astype(BF16), b.astype(BF16), (((1,), (1,)), ((), ())),
                           preferred_element_type=F32)


def _dot_tn(a, b):
    return lax.dot_general(a.astype(BF16), b.astype(BF16), (((0,), (0,)), ((), ())),
                           preferred_element_type=F32)


def _split3(a):
    a1 = a.astype(BF16)
    r1 = a - a1.astype(F32)
    a2 = r1.astype(BF16)
    a3 = (r1 - a2.astype(F32)).astype(BF16)
    return a1, a2, a3


def _dot_sel_l(sel, a):
    a1, a2, a3 = _split3(a)
    s = sel.astype(BF16)
    d = lambda p: jnp.dot(s, p, preferred_element_type=F32)
    return (d(a3) + d(a2)) + d(a1)


def _dot_sel_r(a, sel):
    a1, a2, a3 = _split3(a)
    s = sel.astype(BF16)
    d = lambda p: jnp.dot(p, s, preferred_element_type=F32)
    return (d(a3) + d(a2)) + d(a1)


def _dot_hi(a, b):
    a1 = a.astype(BF16)
    a2 = (a - a1.astype(F32)).astype(BF16)
    b1 = b.astype(BF16)
    b2 = (b - b1.astype(F32)).astype(BF16)
    d = lambda p, q: jnp.dot(p, q, preferred_element_type=F32)
    return (d(a2, b1) + d(a1, b2)) + d(a1, b1)


def _pad_rows(v, rows):
    if v.shape[0] == rows:
        return v
    return jnp.concatenate([v, jnp.zeros((rows - v.shape[0], v.shape[1]), v.dtype)], axis=0)


def _silu(v):
    return v * jax.nn.sigmoid(v)


def _rmsnorm_kernel(x_ref, g_ref, o_ref):
    x = x_ref[...]
    ms = jnp.mean(x * x, axis=-1, keepdims=True)
    o_ref[...] = (x * lax.rsqrt(ms + EPS) * g_ref[...]).astype(o_ref.dtype)


def rmsnorm(x, gain, out_dtype, tm=256):
    m, d = x.shape
    return pl.pallas_call(
        _rmsnorm_kernel,
        grid=(m // tm,),
        in_specs=[pl.BlockSpec((tm, d), lambda i: (i, 0)), pl.BlockSpec((1, d), lambda i: (0, 0))],
        out_specs=pl.BlockSpec((tm, d), lambda i: (i, 0)),
        out_shape=jax.ShapeDtypeStruct((m, d), out_dtype),
        compiler_params=_cparams(("parallel",)),
        name="rmsnorm",
    )(x, gain.reshape(1, d))


def _rmsnorm_router_kernel(x_ref, g_ref, r_ref, o_ref, route_ref):
    x = x_ref[...]
    ms = jnp.mean(x * x, axis=-1, keepdims=True)
    h = x * lax.rsqrt(ms + EPS) * g_ref[...]
    o_ref[...] = h.astype(o_ref.dtype)
    logits = _dot_hi(h, r_ref[...])
    lane = lax.broadcasted_iota(jnp.int32, logits.shape, 1)
    neg = jnp.float32(-jnp.inf)
    lm = jnp.where(lane < N_EXPERTS, logits, neg)
    m1 = jnp.max(lm, axis=-1, keepdims=True)
    i1 = jnp.min(jnp.where(lm == m1, lane, LANES), axis=-1, keepdims=True)
    lm2 = jnp.where(lane == i1, neg, lm)
    m2 = jnp.max(lm2, axis=-1, keepdims=True)
    i2 = jnp.min(jnp.where(lm2 == m2, lane, LANES), axis=-1, keepdims=True)
    e = jnp.exp(m2 - m1)
    p1 = 1.0 / (1.0 + e)
    p2 = e / (1.0 + e)
    route_ref[...] = jnp.where(lane == 0, i1.astype(F32),
                               jnp.where(lane == 1, i2.astype(F32),
                                         jnp.where(lane == 2, p1, jnp.where(lane == 3, p2, 0.0))))


def rmsnorm_router(x, gain, router, tm=256):
    m, d = x.shape
    r_pad = jnp.zeros((d, LANES), F32).at[:, :N_EXPERTS].set(router)
    return pl.pallas_call(
        _rmsnorm_router_kernel,
        grid=(m // tm,),
        in_specs=[pl.BlockSpec((tm, d), lambda i: (i, 0)), pl.BlockSpec((1, d), lambda i: (0, 0)),
                  pl.BlockSpec((d, LANES), lambda i: (0, 0))],
        out_specs=[pl.BlockSpec((tm, d), lambda i: (i, 0)), pl.BlockSpec((tm, LANES), lambda i: (i, 0))],
        out_shape=[jax.ShapeDtypeStruct((m, d), BF16), jax.ShapeDtypeStruct((m, LANES), F32)],
        compiler_params=_cparams(("parallel",)),
        name="rmsnorm_router",
    )(x, gain.reshape(1, d), r_pad)


def _matmul_kernel(*refs, sub, nsub, has_res):
    if has_res:
        x_ref, w_ref, r_ref, o_ref = refs
    else:
        x_ref, w_ref, o_ref = refs
        r_ref = None

    def body(i, carry):
        rows = pl.ds(pl.multiple_of(i * sub, sub), sub)
        acc = jnp.dot(x_ref[rows, :], w_ref[...], preferred_element_type=F32)
        if has_res:
            acc = acc + r_ref[rows, :]
        o_ref[rows, :] = acc.astype(o_ref.dtype)
        return carry

    lax.fori_loop(0, nsub, body, 0)


def matmul(x, w, res=None, *, tm, tn, sub, out_dtype=F32):
    m, k = x.shape
    n = w.shape[1]
    in_specs = [pl.BlockSpec((tm, k), lambda i, j: (i, 0)), pl.BlockSpec((k, tn), lambda i, j: (0, j))]
    args = [x, w]
    if res is not None:
        in_specs.append(pl.BlockSpec((tm, tn), lambda i, j: (i, j)))
        args.append(res)
    return pl.pallas_call(
        functools.partial(_matmul_kernel, sub=sub, nsub=tm // sub, has_res=res is not None),
        grid=(m // tm, n // tn),
        in_specs=in_specs,
        out_specs=pl.BlockSpec((tm, tn), lambda i, j: (i, j)),
        out_shape=jax.ShapeDtypeStruct((m, n), out_dtype),
        compiler_params=_cparams(("parallel", "arbitrary")),
        name="matmul",
    )(*args)


def _ffn_up_kernel(te_ref, tmap_ref, nv_ref, x_ref, wg_ref, wu_ref, o_ref, wg_s, wu_s, *, sub, nsub):
    t = pl.program_id(0)
    nv = nv_ref[t]

    @pl.when(nv > 0)
    def _():
        wg_s[...] = wg_ref[0].astype(BF16)
        wu_s[...] = wu_ref[0].astype(BF16)

        def body(i, carry):
            rows = pl.ds(pl.multiple_of(i * sub, sub), sub)
            xs = x_ref[rows, :]
            g = jnp.dot(xs, wg_s[...], preferred_element_type=F32)
            u = jnp.dot(xs, wu_s[...], preferred_element_type=F32)
            o_ref[rows, :] = (_silu(g) * u).astype(o_ref.dtype)
            return carry

        lax.fori_loop(0, nv, body, 0)

        def zero(i, carry):
            rows = pl.ds(pl.multiple_of(i * sub, sub), sub)
            o_ref[rows, :] = jnp.zeros((sub, o_ref.shape[1]), o_ref.dtype)
            return carry

        lax.fori_loop(nv, nsub, zero, 0)


def ffn_up(x, wg, wu, tile_e, tile_map, tile_nv, *, tm, sub, tn):
    rows, d = x.shape
    nt = rows // tm
    f = wg.shape[2]
    nj = f // tn

    def wmap(t, j, te, tmap, nv):
        return (te[t], 0, jnp.where(nv[t] > 0, j, nj - 1))

    return pl.pallas_call(
        functools.partial(_ffn_up_kernel, sub=sub, nsub=tm // sub),
        grid_spec=pltpu.PrefetchScalarGridSpec(
            num_scalar_prefetch=3,
            grid=(nt, nj),
            in_specs=[pl.BlockSpec((tm, d), lambda t, j, te, tmap, nv: (tmap[t], 0)),
                      pl.BlockSpec((1, d, tn), wmap),
                      pl.BlockSpec((1, d, tn), wmap)],
            out_specs=pl.BlockSpec((tm, tn), lambda t, j, te, tmap, nv: (tmap[t], jnp.where(nv[t] > 0, j, nj - 1))),
            scratch_shapes=[pltpu.VMEM((d, tn), BF16), pltpu.VMEM((d, tn), BF16)],
        ),
        out_shape=jax.ShapeDtypeStruct((rows, f), BF16),
        compiler_params=_cparams(("arbitrary", "arbitrary")),
        name="ffn_up",
    )(tile_e, tile_map, tile_nv, x, wg, wu)


def _ffn_down_kernel(te_ref, tmap_ref, nv_ref, x_ref, w_ref, o_ref, w_s, *, sub):
    t = pl.program_id(0)
    k = pl.program_id(2)
    nv = nv_ref[t]

    @pl.when(nv > 0)
    def _():
        @pl.when(k == 0)
        def _():
            o_ref[...] = jnp.zeros(o_ref.shape, o_ref.dtype)

        w_s[...] = w_ref[0].astype(BF16)

        def body(i, carry):
            rows = pl.ds(pl.multiple_of(i * sub, sub), sub)
            o_ref[rows, :] += jnp.dot(x_ref[rows, :], w_s[...], preferred_element_type=F32)
            return carry

        lax.fori_loop(0, nv, body, 0)


def ffn_down(x, wd, tile_e, tile_map, tile_nv, *, tm, sub, tn, tk):
    rows, f = x.shape
    nt = rows // tm
    d = wd.shape[2]
    nj, nk = d // tn, f // tk

    def live(t, idx, last, nv):
        return jnp.where(nv[t] > 0, idx, last)

    return pl.pallas_call(
        functools.partial(_ffn_down_kernel, sub=sub),
        grid_spec=pltpu.PrefetchScalarGridSpec(
            num_scalar_prefetch=3,
            grid=(nt, nj, nk),
            in_specs=[pl.BlockSpec((tm, tk), lambda t, j, k, te, tmap, nv: (tmap[t], live(t, k, nk - 1, nv))),
                      pl.BlockSpec((1, tk, tn), lambda t, j, k, te, tmap, nv:
                                   (te[t], live(t, k, nk - 1, nv), live(t, j, nj - 1, nv)))],
            out_specs=pl.BlockSpec((tm, tn), lambda t, j, k, te, tmap, nv: (tmap[t], live(t, j, nj - 1, nv))),
            scratch_shapes=[pltpu.VMEM((tk, tn), BF16)],
        ),
        out_shape=jax.ShapeDtypeStruct((rows, d), F32),
        compiler_params=_cparams(("arbitrary", "arbitrary", "arbitrary")),
        name="ffn_down",
    )(tile_e, tile_map, tile_nv, x, wd)


def _ssd_kernel(x_ref, b_ref, c_ref, dt_ref, dtt_ref, z_ref, alog_ref, alogt_ref, dexp_ref, norm_ref, h0_ref,
                y_ref, h_ref, *, tq):
    T = T_PAD

    @pl.when(pl.program_id(1) == 0)
    def _():
        h_ref[...] = h0_ref[...]

    row = lax.broadcasted_iota(jnp.int32, (T, T), 0)
    col = lax.broadcasted_iota(jnp.int32, (T, T), 1)
    incl = (row >= col)
    tri_l = incl.astype(F32)
    tri_u = (row <= col).astype(F32)

    dt = _pad_rows(dt_ref[...], T)
    a = dt * (-jnp.exp(alog_ref[...]))
    at = dtt_ref[0] * (-jnp.exp(alogt_ref[...]))
    acs = _dot_sel_l(tri_l, a)
    acst = _dot_sel_r(at, tri_u)
    last = acs[T - 1:T, :]
    e_last = jnp.exp(last)
    wts = jnp.exp(last - acs) * dt

    hp = lax.broadcasted_iota(jnp.int32, (SSM_HEADS, D_SSM), 1) // SSM_HEAD_DIM
    expand = (hp == lax.broadcasted_iota(jnp.int32, (SSM_HEADS, D_SSM), 0)).astype(F32)
    dt_e = _dot_sel_r(dt, expand)
    wts_e = _dot_sel_r(wts, expand)
    eacs_e = _dot_sel_r(jnp.exp(acs[:tq]), expand)

    x = _pad_rows(x_ref[...], T)
    xdt = (x * dt_e).astype(BF16)
    xw = (x * wts_e).astype(BF16)
    incl_q = incl[:tq]
    half = lax.broadcasted_iota(jnp.int32, (tq, LANES), 1) < SSM_HEAD_DIM
    gw = SSM_HPG * SSM_HEAD_DIM

    y_groups = []
    for g in range(SSM_GROUPS):
        bg = _pad_rows(b_ref[:, g * D_STATE:(g + 1) * D_STATE], T).astype(BF16)
        cg = c_ref[:, g * D_STATE:(g + 1) * D_STATE].astype(BF16)
        cb = _dot_nt(cg, bg)
        hg = h_ref[0, g * gw:(g + 1) * gw, :]
        y_state = _dot_nt(cg, hg)
        pieces = []
        for j in range(SSM_HPG // 2):
            ys = []
            for hh in (2 * j, 2 * j + 1):
                h = g * SSM_HPG + hh
                seg = acs[:tq, h:h + 1] - acst[h:h + 1, :]
                w = cb * jnp.exp(jnp.where(incl_q, seg, -jnp.inf))
                lo = g * gw + j * LANES
                ys.append(jnp.dot(w.astype(BF16), xdt[:, lo:lo + LANES], preferred_element_type=F32))
            pieces.append(jnp.where(half, ys[0], ys[1]))
        y_intra = jnp.concatenate(pieces, axis=1)
        y_groups.append(y_intra + eacs_e[:, g * gw:(g + 1) * gw] * y_state)
        upd = _dot_tn(xw[:, g * gw:(g + 1) * gw], bg)
        for hh in range(SSM_HPG):
            h = g * SSM_HPG + hh
            r0 = g * gw + hh * SSM_HEAD_DIM
            h_ref[0, r0:r0 + SSM_HEAD_DIM, :] = (e_last[:, h:h + 1] * hg[hh * SSM_HEAD_DIM:(hh + 1) * SSM_HEAD_DIM, :]
                                                 + upd[hh * SSM_HEAD_DIM:(hh + 1) * SSM_HEAD_DIM, :])

    y = jnp.concatenate(y_groups, axis=1)
    y = y + dexp_ref[...] * x[:tq]
    y = y * _silu(z_ref[...])
    outs = []
    for g in range(SSM_GROUPS):
        yg = y[:, g * gw:(g + 1) * gw]
        ms = jnp.mean(yg * yg, axis=-1, keepdims=True)
        outs.append(yg * lax.rsqrt(ms + EPS))
    y_ref[...] = (jnp.concatenate(outs, axis=1) * norm_ref[...]).astype(y_ref.dtype)


def ssd_scan(x, bm, cm, dt, dtt, z, a_log, d_skip, norm, h0, *, nb, nc, tq):
    rows = nb * nc * tq
    blk = lambda width: pl.BlockSpec((tq, width), lambda b, c: (b * nc + c, 0))
    full = lambda shape: pl.BlockSpec(shape, lambda b, c: tuple(0 for _ in shape))
    state = pl.BlockSpec((1, D_SSM, D_STATE), lambda b, c: (b, 0, 0))
    return pl.pallas_call(
        functools.partial(_ssd_kernel, tq=tq),
        grid=(nb, nc),
        in_specs=[blk(D_SSM), blk(SSM_GN), blk(SSM_GN), blk(SSM_HEADS),
                  pl.BlockSpec((1, SSM_HEADS, T_PAD), lambda b, c: (b * nc + c, 0, 0)),
                  blk(D_SSM), full((1, SSM_HEADS)), full((SSM_HEADS, 1)), full((1, D_SSM)), full((1, D_SSM)), state],
        out_specs=[blk(D_SSM), state],
        out_shape=[jax.ShapeDtypeStruct((rows, D_SSM), BF16), jax.ShapeDtypeStruct((nb, D_SSM, D_STATE), F32)],
        compiler_params=_cparams(("arbitrary", "arbitrary")),
        name="ssd_scan",
    )(x, bm, cm, dt, dtt, z, a_log.reshape(1, SSM_HEADS), a_log.reshape(SSM_HEADS, 1),
      jnp.repeat(d_skip, SSM_HEAD_DIM).reshape(1, D_SSM), norm.reshape(1, D_SSM), h0)


def _gdn_kernel(q_ref, k_ref, v_ref, gate_ref, g_ref, gt_ref, beta_ref, norm_ref, s0_ref, o_ref, s_ref,
                *, tq, n_valid):
    T = T_PAD
    K = GDN_HEAD_DIM

    @pl.when(pl.program_id(1) == 0)
    def _():
        s_ref[...] = s0_ref[...]

    row = lax.broadcasted_iota(jnp.int32, (T, T), 0)
    col = lax.broadcasted_iota(jnp.int32, (T, T), 1)
    incl = row >= col
    strict = row > col
    eye = (row == col).astype(F32)
    tri_l = incl.astype(F32)
    tri_u = (row <= col).astype(F32)

    g = _pad_rows(g_ref[...], T)
    beta = _pad_rows(beta_ref[...], T)
    gcs = _dot_sel_l(tri_l, g)
    gcst = _dot_sel_r(gt_ref[0], tri_u)
    egcs = jnp.exp(gcs)
    last = gcs[T - 1:T, :]
    kdec = jnp.exp(last - gcs)
    cd = jnp.exp(last)

    n_sq = 0
    while (1 << (n_sq + 1)) < n_valid:
        n_sq += 1

    for h in range(GDN_HEADS):
        lanes = slice(h * K, (h + 1) * K)
        qh = q_ref[:, lanes]
        kh = k_ref[:, lanes]
        qh = qh * lax.rsqrt(jnp.sum(qh * qh, axis=-1, keepdims=True) + EPS) * (K ** -0.5)
        kh = kh * lax.rsqrt(jnp.sum(kh * kh, axis=-1, keepdims=True) + EPS)
        kh = _pad_rows(kh, T)
        vh = _pad_rows(v_ref[:, lanes], T)
        bcol = beta[:, h:h + 1]
        decay = jnp.exp(jnp.where(incl, gcs[:, h:h + 1] - gcst[h:h + 1, :], -jnp.inf))
        kk = _dot_nt(kh, kh)
        a_mat = jnp.where(strict, bcol * kk * decay, 0.0)
        minv = eye - a_mat
        pw = a_mat
        for _ in range(n_sq):
            pw = _dot_hi(pw, pw)
            minv = minv + _dot_hi(minv, pw)
        rhs = jnp.concatenate([vh * bcol, kh * (bcol * egcs[:, h:h + 1])], axis=1)
        sol = _dot_hi(minv, rhs)
        u, w = sol[:, :K], sol[:, K:]
        s = s_ref[0, h]
        v_new = u - _dot(w, s)
        qk = _dot_nt(qh, kh) * decay[:tq]
        o = _dot(qh * egcs[:tq, h:h + 1], s) + _dot(qk, v_new)
        s_ref[0, h] = cd[:, h:h + 1] * s + _dot_tn(kh * kdec[:, h:h + 1], v_new)
        ms = jnp.mean(o * o, axis=-1, keepdims=True)
        o = o * lax.rsqrt(ms + EPS) * norm_ref[...] * _silu(gate_ref[:, lanes])
        o_ref[:, lanes] = o.astype(o_ref.dtype)


def gdn_scan(q, k, v, gate, g, gt, beta, norm, s0, *, nb, nc, tq, n_valid):
    rows = nb * nc * tq
    blk = lambda width: pl.BlockSpec((tq, width), lambda b, c: (b * nc + c, 0))
    state = pl.BlockSpec((1, GDN_HEADS, GDN_HEAD_DIM, GDN_HEAD_DIM), lambda b, c: (b, 0, 0, 0))
    return pl.pallas_call(
        functools.partial(_gdn_kernel, tq=tq, n_valid=n_valid),
        grid=(nb, nc),
        in_specs=[blk(D_GDN), blk(D_GDN), blk(D_GDN), blk(D_GDN), blk(GDN_HEADS),
                  pl.BlockSpec((1, GDN_HEADS, T_PAD), lambda b, c: (b * nc + c, 0, 0)),
                  blk(GDN_HEADS), pl.BlockSpec((1, GDN_HEAD_DIM), lambda b, c: (0, 0)), state],
        out_specs=[blk(D_GDN), state],
        out_shape=[jax.ShapeDtypeStruct((rows, D_GDN), BF16),
                   jax.ShapeDtypeStruct((nb, GDN_HEADS, GDN_HEAD_DIM, GDN_HEAD_DIM), F32)],
        compiler_params=_cparams(("arbitrary", "arbitrary")),
        name="gdn_scan",
    )(q, k, v, gate, g, gt, beta, norm.reshape(1, GDN_HEAD_DIM), s0)


def _short_conv(x, buf, w):
    seq = x.shape[1]
    xp = jnp.concatenate([buf, x], axis=1)
    y = w[0] * xp[:, 0:seq]
    for j in range(1, CONV_K):
        y = y + w[j] * xp[:, j:j + seq]
    return y, xp[:, seq:]


def _chunk_layout(t, tq):
    b, seq, c = t.shape
    q = min(CHUNK, seq)
    nc = seq // q
    t = t.reshape(b, nc, q, c)
    if q < tq:
        t = jnp.pad(t, ((0, 0), (0, 0), (0, tq - q), (0, 0)))
    return t.reshape(b * nc * tq, c)


def _chunk_transposed(t, tq):
    rows, h = t.shape
    t = jnp.swapaxes(t.reshape(rows // tq, tq, h), 1, 2)
    if tq < T_PAD:
        t = jnp.pad(t, ((0, 0), (0, 0), (0, T_PAD - tq)))
    return t


def _unchunk(t, b, seq, tq):
    q = min(CHUNK, seq)
    nc = seq // q
    return t.reshape(b, nc, tq, -1)[:, :, :q].reshape(b * seq, -1)


def _mixers(proj, bsz, seq, st_ssm, st_ssm_conv, st_gdn, st_gdn_conv, p):
    q = min(CHUNK, seq)
    nc = seq // q
    tq = q if q == CHUNK else DEC_ROWS
    o_xbc, o_qkv = D_SSM, D_SSM + SSM_CONV_DIM
    o_gate = o_qkv + GDN_CONV_DIM
    o_small = o_gate + D_GDN
    pr = proj.reshape(bsz, seq, -1)
    z = pr[..., :D_SSM]
    xbc, ssm_buf_new = _short_conv(pr[..., o_xbc:o_qkv], st_ssm_conv, p['ssm_conv_w'])
    xbc = jax.nn.silu(xbc + p['ssm_conv_b'])
    dt = jax.nn.softplus(pr[..., o_small:o_small + SSM_HEADS] + p['ssm_dt_bias'])
    qkv, gdn_buf_new = _short_conv(pr[..., o_qkv:o_gate], st_gdn_conv, p['gdn_conv_w'])
    qkv = jax.nn.silu(qkv)
    gate = pr[..., o_gate:o_small]
    beta = jax.nn.sigmoid(pr[..., o_small + SSM_HEADS:o_small + SSM_HEADS + GDN_HEADS])
    a_raw = pr[..., o_small + SSM_HEADS + GDN_HEADS:o_small + SSM_HEADS + 2 * GDN_HEADS]
    g = -jnp.exp(p['gdn_a_log']) * jax.nn.softplus(a_raw + p['gdn_dt_bias'])

    lay = lambda t: _chunk_layout(t, tq)
    dt_l = lay(dt)
    y, ssm_new = ssd_scan(lay(xbc[..., :D_SSM]), lay(xbc[..., D_SSM:D_SSM + SSM_GN]), lay(xbc[..., D_SSM + SSM_GN:]),
                          dt_l, _chunk_transposed(dt_l, tq), lay(z), p['ssm_a_log'], p['ssm_d'], p['ssm_norm'],
                          st_ssm.reshape(bsz, D_SSM, D_STATE), nb=bsz, nc=nc, tq=tq)
    g_l = lay(g)
    o, gdn_new = gdn_scan(lay(qkv[..., :D_GDN]), lay(qkv[..., D_GDN:2 * D_GDN]), lay(qkv[..., 2 * D_GDN:]), lay(gate),
                          g_l, _chunk_transposed(g_l, tq), lay(beta), p['gdn_norm'], st_gdn,
                          nb=bsz, nc=nc, tq=tq, n_valid=q)
    mixed = jnp.concatenate([_unchunk(y, bsz, seq, tq), _unchunk(o, bsz, seq, tq)], axis=-1)
    return (mixed, ssm_new.reshape(bsz, SSM_HEADS, SSM_HEAD_DIM, D_STATE), ssm_buf_new, gdn_new, gdn_buf_new)


def _reorder_w_in(w):
    small = jnp.concatenate([w[:, SPLIT_XBC:SPLIT_DT], w[:, SPLIT_GATE:]], axis=1)
    small = jnp.pad(small, ((0, 0), (0, LANES - small.shape[1])))
    return jnp.concatenate([w[:, :SPLIT_XBC], w[:, SPLIT_DT:SPLIT_GATE], small], axis=1).astype(BF16)


def _dense_ffn(h, x, wg, wu, wd):
    m = h.shape[0]
    tm, sub = 1088, 272
    nt = m // tm
    te = jnp.zeros((nt,), jnp.int32)
    tmap = jnp.arange(nt, dtype=jnp.int32)
    nv = jnp.full((nt,), tm // sub, jnp.int32)
    hid = ffn_up(h, wg, wu, te, tmap, nv, tm=tm, sub=sub, tn=256)
    y = ffn_down(hid, wd, te, tmap, nv, tm=tm, sub=sub, tn=2048, tk=512)
    return x + y


def _moe_ffn(h, x, route, wg, wu, wd):
    m = h.shape[0]
    tm, sub = 1024, 256
    nt = (m * TOP_K) // tm + N_EXPERTS
    top_i = route[:, :TOP_K].astype(jnp.int32)
    probs = route[:, TOP_K:2 * TOP_K]
    e_flat = top_i.reshape(-1)
    n_assign = e_flat.shape[0]
    order = jnp.argsort(e_flat, stable=True).astype(jnp.int32)
    counts = jnp.sum(e_flat[:, None] == jnp.arange(N_EXPERTS, dtype=jnp.int32)[None, :], axis=0).astype(jnp.int32)
    tiles_per_e = (counts + tm - 1) // tm
    tile_end_e = jnp.cumsum(tiles_per_e)
    tile_start_e = tile_end_e - tiles_per_e
    group_start = jnp.cumsum(counts) - counts
    e_sorted = e_flat[order]
    rank = jnp.arange(n_assign, dtype=jnp.int32) - group_start[e_sorted]
    pos_sorted = tile_start_e[e_sorted] * tm + rank
    src = jnp.zeros((nt * tm,), jnp.int32).at[pos_sorted].set(order // TOP_K)
    pos = jnp.zeros((n_assign,), jnp.int32).at[order].set(pos_sorted)
    n_used = tile_end_e[-1]
    tile_ids = jnp.arange(nt, dtype=jnp.int32)
    tmap = jnp.minimum(tile_ids, n_used - 1)
    te = jnp.minimum(jnp.searchsorted(tile_end_e, tmap, side='right'), N_EXPERTS - 1).astype(jnp.int32)
    rows_in_tile = jnp.clip(counts[te] - (tmap - tile_start_e[te]) * tm, 0, tm)
    nv = jnp.where(tile_ids < n_used, (rows_in_tile + sub - 1) // sub, 0).astype(jnp.int32)

    xs = jnp.take(h, src, axis=0)
    hid = ffn_up(xs, wg, wu, te, tmap, nv, tm=tm, sub=sub, tn=256)
    ys = ffn_down(hid, wd, te, tmap, nv, tm=tm, sub=sub, tn=2048, tk=512)
    pos = pos.reshape(m, TOP_K)
    out = probs[:, 0:1] * jnp.take(ys, pos[:, 0], axis=0)
    out = out + probs[:, 1:2] * jnp.take(ys, pos[:, 1], axis=0)
    return x + out


def kernel(x_prompt, x_sample, state_ssm, state_ssm_conv, state_gdn, state_gdn_conv, norm_mix, w_in, ssm_conv_w, ssm_conv_b, ssm_dt_bias, ssm_a_log, ssm_d, ssm_norm, gdn_conv_w, gdn_dt_bias, gdn_a_log, gdn_norm, w_out, norm_ffn, dense_w_gate, dense_w_up, dense_w_down, moe_router, moe_w_gate, moe_w_up, moe_w_down, norm_final):
    bp, lp, d = x_prompt.shape
    bs, ls, _ = x_sample.shape
    mp, ms = bp * lp, bs * ls
    x = jnp.concatenate([x_prompt.reshape(mp, d), x_sample.reshape(ms, d)], axis=0)
    m = mp + ms
    tm_all = m // 8

    new_p = [[], [], [], []]
    new_s = [[], [], [], []]
    for i in range(DEPTH):
        p = dict(ssm_conv_w=ssm_conv_w[i], ssm_conv_b=ssm_conv_b[i], ssm_dt_bias=ssm_dt_bias[i],
                 ssm_a_log=ssm_a_log[i], ssm_d=ssm_d[i], ssm_norm=ssm_norm[i], gdn_conv_w=gdn_conv_w[i],
                 gdn_dt_bias=gdn_dt_bias[i], gdn_a_log=gdn_a_log[i], gdn_norm=gdn_norm[i])
        h = rmsnorm(x, norm_mix[i], BF16)
        proj = matmul(h, _reorder_w_in(w_in[i]), tm=tm_all, tn=640, sub=tm_all // 4)
        zeros_p = (jnp.zeros((bp, SSM_HEADS, SSM_HEAD_DIM, D_STATE), F32), jnp.zeros((bp, CONV_K - 1, SSM_CONV_DIM), F32),
                   jnp.zeros((bp, GDN_HEADS, GDN_HEAD_DIM, GDN_HEAD_DIM), F32), jnp.zeros((bp, CONV_K - 1, GDN_CONV_DIM), F32))
        res_p = _mixers(proj[:mp], bp, lp, *zeros_p, p)
        res_s = _mixers(proj[mp:], bs, ls, state_ssm[i], state_ssm_conv[i], state_gdn[i], state_gdn_conv[i], p)
        for k in range(4):
            new_p[k].append(res_p[k + 1])
            new_s[k].append(res_s[k + 1])
        mixed = jnp.concatenate([res_p[0], res_s[0]], axis=0)
        x = matmul(mixed, w_out[i].astype(BF16), x, tm=tm_all, tn=512, sub=tm_all // 4)
        j = i // 2
        if i % 2 == 0:
            h = rmsnorm(x, norm_ffn[i], BF16)
            x = _dense_ffn(h, x, dense_w_gate[j:j + 1], dense_w_up[j:j + 1], dense_w_down[j:j + 1])
        else:
            h, route = rmsnorm_router(x, norm_ffn[i], moe_router[j])
            x = _moe_ffn(h, x, route, moe_w_gate[j], moe_w_up[j], moe_w_down[j])
    y = rmsnorm(x, norm_final, F32)
    return (y[:mp].reshape(bp, lp, d), y[mp:].reshape(bs, ls, d),
            jnp.stack(new_p[0]), jnp.stack(new_p[1]), jnp.stack(new_p[2]), jnp.stack(new_p[3]),
            jnp.stack(new_s[0]), jnp.stack(new_s[1]), jnp.stack(new_s[2]), jnp.stack(new_s[3]))
```

```python
import functools

import jax
import jax.numpy as jnp
from jax import lax
from jax.experimental import pallas as pl
from jax.experimental.pallas import tpu as pltpu

F32 = jnp.float32
BF16 = jnp.bfloat16

D_MODEL = 4096
DEPTH = 2
D_SSM = 2048
D_GDN = 2048
SSM_HEAD_DIM = 64
SSM_HEADS = 32
SSM_GROUPS = 4
SSM_HPG = 8
D_STATE = 128
SSM_GN = SSM_GROUPS * D_STATE
SSM_CONV_DIM = D_SSM + 2 * SSM_GN
GDN_HEAD_DIM = 128
GDN_HEADS = 16
GDN_CONV_DIM = 3 * D_GDN
CONV_K = 4
CHUNK = 128
SPLIT_Z = D_SSM
SPLIT_XBC = SPLIT_Z + SSM_CONV_DIM
SPLIT_DT = SPLIT_XBC + SSM_HEADS
SPLIT_QKV = SPLIT_DT + GDN_CONV_DIM
SPLIT_GATE = SPLIT_QKV + D_GDN
SPLIT_BETA = SPLIT_GATE + GDN_HEADS
IN_COLS = SPLIT_BETA + GDN_HEADS
D_FF = 14336
N_EXPERTS = 8
TOP_K = 2
EPS = 1e-6

P_Z, P_X, P_Q, P_K, P_V, P_GATE = 0, 2048, 4096, 6144, 8192, 10240
P_B, P_C, P_SMALL, P_COLS = 12288, 12800, 13312, 13440
S_DT, S_BETA, S_A = 0, SSM_HEADS, SSM_HEADS + GDN_HEADS
CONV_HIST = 8

LANES = 128
T_PAD = 128
DEC_ROWS = 16
VMEM_LIMIT = 56 * 1024 * 1024


def _cparams(sem):
    return pltpu.CompilerParams(dimension_semantics=sem, vmem_limit_bytes=VMEM_LIMIT)


def _dot(a, b):
    return jnp.dot(a.astype(BF16), b.astype(BF16), preferred_element_type=F32)


def _dot_nt(a, b):
    return lax.dot_general(a.astype(BF16), b.astype(BF16), (((1,), (1,)), ((), ())),
                           preferred_element_type=F32)


def _dot_tn(a, b):
    return lax.dot_general(a.astype(BF16), b.astype(BF16), (((0,), (0,)), ((), ())),
                           preferred_element_type=F32)


def _split3(a):
    a1 = a.astype(BF16)
    r1 = a - a1.astype(F32)
    a2 = r1.astype(BF16)
    a3 = (r1 - a2.astype(F32)).astype(BF16)
    return a1, a2, a3


def _dot_sel_l(sel, a):
    a1, a2, a3 = _split3(a)
    s = sel.astype(BF16)
    d = lambda p: jnp.dot(s, p, preferred_element_type=F32)
    return (d(a3) + d(a2)) + d(a1)


def _dot_sel_r(a, sel):
    a1, a2, a3 = _split3(a)
    s = sel.astype(BF16)
    d = lambda p: jnp.dot(p, s, preferred_element_type=F32)
    return (d(a3) + d(a2)) + d(a1)


def _dot_hi(a, b):
    a1 = a.astype(BF16)
    a2 = (a - a1.astype(F32)).astype(BF16)
    b1 = b.astype(BF16)
    b2 = (b - b1.astype(F32)).astype(BF16)
    d = lambda p, q: jnp.dot(p, q, preferred_element_type=F32)
    return (d(a2, b1) + d(a1, b2)) + d(a1, b1)


def _pad_rows(v, rows):
    if v.shape[0] == rows:
        return v
    return jnp.concatenate([v, jnp.zeros((rows - v.shape[0], v.shape[1]), v.dtype)], axis=0)


def _silu(v):
    return v * jax.nn.sigmoid(v)


def _causal_conv(raw_ref, buf_ref, w_ref, tq, first_chunk):
    @pl.when(first_chunk)
    def _():
        buf_ref[0:CONV_HIST, :] = jnp.zeros((CONV_HIST, buf_ref.shape[1]), F32)

    buf_ref[CONV_HIST:CONV_HIST + tq, :] = raw_ref[...]
    w = w_ref[...]
    lo = CONV_HIST - (CONV_K - 1)
    y = w[0:1] * buf_ref[lo:lo + tq, :]
    for j in range(1, CONV_K):
        y = y + w[j:j + 1] * buf_ref[lo + j:lo + j + tq, :]
    buf_ref[0:CONV_HIST, :] = buf_ref[tq:tq + CONV_HIST, :]
    return y


def _small_and_transpose(small_ref):
    raw = _pad_rows(small_ref[...], T_PAD)
    row = lax.broadcasted_iota(jnp.int32, (T_PAD, T_PAD), 0)
    col = lax.broadcasted_iota(jnp.int32, (T_PAD, T_PAD), 1)
    eye = (row == col).astype(BF16)
    a1, a2, a3 = _split3(raw)
    d = lambda p: lax.dot_general(p, eye, (((0,), (0,)), ((), ())), preferred_element_type=F32)
    return raw, (d(a3) + d(a2)) + d(a1)


def _valid_masks(valid, width):
    lo, hi = valid
    r = lax.broadcasted_iota(jnp.int32, (T_PAD, width), 0)
    c = lax.broadcasted_iota(jnp.int32, (width, T_PAD), 1)
    return (r >= lo) & (r < hi), (c >= lo) & (c < hi)


def _rmsnorm_kernel(x_ref, g_ref, o_ref):
    x = x_ref[...]
    ms = jnp.mean(x * x, axis=-1, keepdims=True)
    o_ref[...] = (x * lax.rsqrt(ms + EPS) * g_ref[...]).astype(o_ref.dtype)


def rmsnorm(x, gain, out_dtype, tm=256):
    m, d = x.shape
    return pl.pallas_call(
        _rmsnorm_kernel,
        grid=(m // tm,),
        in_specs=[pl.BlockSpec((tm, d), lambda i: (i, 0)), pl.BlockSpec((1, d), lambda i: (0, 0))],
        out_specs=pl.BlockSpec((tm, d), lambda i: (i, 0)),
        out_shape=jax.ShapeDtypeStruct((m, d), out_dtype),
        compiler_params=_cparams(("parallel",)),
        name="rmsnorm",
    )(x, gain.reshape(1, d))


def _rmsnorm_router_kernel(x_ref, g_ref, r_ref, o_ref, route_ref):
    x = x_ref[...]
    ms = jnp.mean(x * x, axis=-1, keepdims=True)
    h = x * lax.rsqrt(ms + EPS) * g_ref[...]
    o_ref[...] = h.astype(o_ref.dtype)
    logits = _dot_hi(h, r_ref[...])
    lane = lax.broadcasted_iota(jnp.int32, logits.shape, 1)
    neg = jnp.float32(-jnp.inf)
    lm = jnp.where(lane < N_EXPERTS, logits, neg)
    m1 = jnp.max(lm, axis=-1, keepdims=True)
    i1 = jnp.min(jnp.where(lm == m1, lane, LANES), axis=-1, keepdims=True)
    lm2 = jnp.where(lane == i1, neg, lm)
    m2 = jnp.max(lm2, axis=-1, keepdims=True)
    i2 = jnp.min(jnp.where(lm2 == m2, lane, LANES), axis=-1, keepdims=True)
    e = jnp.exp(m2 - m1)
    p1 = 1.0 / (1.0 + e)
    p2 = e / (1.0 + e)
    route_ref[...] = jnp.where(lane == 0, i1.astype(F32),
                               jnp.where(lane == 1, i2.astype(F32),
                                         jnp.where(lane == 2, p1, jnp.where(lane == 3, p2, 0.0))))


def rmsnorm_router(x, gain, router, tm=256):
    m, d = x.shape
    r_pad = jnp.zeros((d, LANES), F32).at[:, :N_EXPERTS].set(router)
    return pl.pallas_call(
        _rmsnorm_router_kernel,
        grid=(m // tm,),
        in_specs=[pl.BlockSpec((tm, d), lambda i: (i, 0)), pl.BlockSpec((1, d), lambda i: (0, 0)),
                  pl.BlockSpec((d, LANES), lambda i: (0, 0))],
        out_specs=[pl.BlockSpec((tm, d), lambda i: (i, 0)), pl.BlockSpec((tm, LANES), lambda i: (i, 0))],
        out_shape=[jax.ShapeDtypeStruct((m, d), BF16), jax.ShapeDtypeStruct((m, LANES), F32)],
        compiler_params=_cparams(("parallel",)),
        name="rmsnorm_router",
    )(x, gain.reshape(1, d), r_pad)


def _matmul_kernel(*refs, sub, nsub, has_res):
    if has_res:
        x_ref, w_ref, r_ref, o_ref = refs
    else:
        x_ref, w_ref, o_ref = refs
        r_ref = None

    def body(i, carry):
        rows = pl.ds(pl.multiple_of(i * sub, sub), sub)
        acc = jnp.dot(x_ref[rows, :], w_ref[...], preferred_element_type=F32)
        if has_res:
            acc = acc + r_ref[rows, :]
        o_ref[rows, :] = acc.astype(o_ref.dtype)
        return carry

    lax.fori_loop(0, nsub, body, 0)


def matmul(x, w, res=None, *, tm, tn, sub, out_dtype=F32):
    m, k = x.shape
    n = w.shape[1]
    in_specs = [pl.BlockSpec((tm, k), lambda i, j: (i, 0)), pl.BlockSpec((k, tn), lambda i, j: (0, j))]
    args = [x, w]
    if res is not None:
        in_specs.append(pl.BlockSpec((tm, tn), lambda i, j: (i, j)))
        args.append(res)
    return pl.pallas_call(
        functools.partial(_matmul_kernel, sub=sub, nsub=tm // sub, has_res=res is not None),
        grid=(m // tm, n // tn),
        in_specs=in_specs,
        out_specs=pl.BlockSpec((tm, tn), lambda i, j: (i, j)),
        out_shape=jax.ShapeDtypeStruct((m, n), out_dtype),
        compiler_params=_cparams(("parallel", "arbitrary")),
        name="matmul",
    )(*args)


def _ffn_up_kernel(te_ref, tmap_ref, nv_ref, x_ref, wg_ref, wu_ref, o_ref, *, sub, nsub):
    t = pl.program_id(0)
    nv = nv_ref[t]

    @pl.when(nv > 0)
    def _():
        def body(i, carry):
            rows = pl.ds(pl.multiple_of(i * sub, sub), sub)
            xs = x_ref[rows, :]
            g = jnp.dot(xs, wg_ref[0].astype(BF16), preferred_element_type=F32)
            u = jnp.dot(xs, wu_ref[0].astype(BF16), preferred_element_type=F32)
            o_ref[rows, :] = (_silu(g) * u).astype(o_ref.dtype)
            return carry

        lax.fori_loop(0, nv, body, 0)

        def zero(i, carry):
            rows = pl.ds(pl.multiple_of(i * sub, sub), sub)
            o_ref[rows, :] = jnp.zeros((sub, o_ref.shape[1]), o_ref.dtype)
            return carry

        lax.fori_loop(nv, nsub, zero, 0)


def ffn_up(x, wg, wu, tile_e, tile_map, tile_nv, *, tm, sub, tn):
    rows, d = x.shape
    nt = rows // tm
    f = wg.shape[2]
    nj = f // tn

    def wmap(t, j, te, tmap, nv):
        return (te[t], 0, jnp.where(nv[t] > 0, j, nj - 1))

    return pl.pallas_call(
        functools.partial(_ffn_up_kernel, sub=sub, nsub=tm // sub),
        grid_spec=pltpu.PrefetchScalarGridSpec(
            num_scalar_prefetch=3,
            grid=(nt, nj),
            in_specs=[pl.BlockSpec((tm, d), lambda t, j, te, tmap, nv: (tmap[t], 0)),
                      pl.BlockSpec((1, d, tn), wmap),
                      pl.BlockSpec((1, d, tn), wmap)],
            out_specs=pl.BlockSpec((tm, tn), lambda t, j, te, tmap, nv: (tmap[t], jnp.where(nv[t] > 0, j, nj - 1))),
        ),
        out_shape=jax.ShapeDtypeStruct((rows, f), BF16),
        compiler_params=_cparams(("arbitrary", "arbitrary")),
        name="ffn_up",
    )(tile_e, tile_map, tile_nv, x, wg, wu)


def _ffn_down_kernel(te_ref, tmap_ref, nv_ref, x_ref, w_ref, o_ref, *, sub):
    t = pl.program_id(0)
    k = pl.program_id(2)
    nv = nv_ref[t]

    @pl.when(nv > 0)
    def _():
        @pl.when(k == 0)
        def _():
            o_ref[...] = jnp.zeros(o_ref.shape, o_ref.dtype)

        def body(i, carry):
            rows = pl.ds(pl.multiple_of(i * sub, sub), sub)
            o_ref[rows, :] += jnp.dot(x_ref[rows, :], w_ref[0].astype(BF16), preferred_element_type=F32)
            return carry

        lax.fori_loop(0, nv, body, 0)


def ffn_down(x, wd, tile_e, tile_map, tile_nv, *, tm, sub, tn, tk):
    rows, f = x.shape
    nt = rows // tm
    d = wd.shape[2]
    nj, nk = d // tn, f // tk

    def live(t, idx, last, nv):
        return jnp.where(nv[t] > 0, idx, last)

    return pl.pallas_call(
        functools.partial(_ffn_down_kernel, sub=sub),
        grid_spec=pltpu.PrefetchScalarGridSpec(
            num_scalar_prefetch=3,
            grid=(nt, nj, nk),
            in_specs=[pl.BlockSpec((tm, tk), lambda t, j, k, te, tmap, nv: (tmap[t], live(t, k, nk - 1, nv))),
                      pl.BlockSpec((1, tk, tn), lambda t, j, k, te, tmap, nv:
                                   (te[t], live(t, k, nk - 1, nv), live(t, j, nj - 1, nv)))],
            out_specs=pl.BlockSpec((tm, tn), lambda t, j, k, te, tmap, nv: (tmap[t], live(t, j, nj - 1, nv))),
        ),
        out_shape=jax.ShapeDtypeStruct((rows, d), F32),
        compiler_params=_cparams(("arbitrary", "arbitrary", "arbitrary")),
        name="ffn_down",
    )(tile_e, tile_map, tile_nv, x, wd)


def _ssd_kernel(z_ref, xraw_ref, braw_ref, craw_ref, small_ref, wx_ref, wb_ref, wc_ref, bx_ref, bb_ref, bc_ref,
                dtb_ref, dtbt_ref, alog_ref, alogt_ref, dexp_ref, norm_ref, h0_ref,
                y_ref, h_ref, bufx, bufb, bufc, *, tq, valid):
    T = T_PAD

    @pl.when(pl.program_id(1) == 0)
    def _():
        h_ref[...] = h0_ref[...]

    row = lax.broadcasted_iota(jnp.int32, (T, T), 0)
    col = lax.broadcasted_iota(jnp.int32, (T, T), 1)
    incl = (row >= col)
    tri_l = incl.astype(F32)
    tri_u = (row <= col).astype(F32)

    first_chunk = pl.program_id(1) == 0
    x_act = _silu(_causal_conv(xraw_ref, bufx, wx_ref, tq, first_chunk) + bx_ref[...])
    b_act = _silu(_causal_conv(braw_ref, bufb, wb_ref, tq, first_chunk) + bb_ref[...])
    c_act = _silu(_causal_conv(craw_ref, bufc, wc_ref, tq, first_chunk) + bc_ref[...])
    raw, raw_t = _small_and_transpose(small_ref)
    rmask, cmask = _valid_masks(valid, SSM_HEADS)
    dt = jnp.where(rmask, jax.nn.softplus(raw[:, S_DT:S_DT + SSM_HEADS] + dtb_ref[...]), 0.0)
    dtt = jnp.where(cmask, jax.nn.softplus(raw_t[S_DT:S_DT + SSM_HEADS, :] + dtbt_ref[...]), 0.0)
    a = dt * (-jnp.exp(alog_ref[...]))
    at = dtt * (-jnp.exp(alogt_ref[...]))
    acs = _dot_sel_l(tri_l, a)
    acst = _dot_sel_r(at, tri_u)
    last = acs[T - 1:T, :]
    e_last = jnp.exp(last)
    wts = jnp.exp(last - acs) * dt

    hp = lax.broadcasted_iota(jnp.int32, (SSM_HEADS, D_SSM), 1) // SSM_HEAD_DIM
    expand = (hp == lax.broadcasted_iota(jnp.int32, (SSM_HEADS, D_SSM), 0)).astype(F32)
    dt_e = _dot_sel_r(dt, expand)
    wts_e = _dot_sel_r(wts, expand)
    eacs_e = _dot_sel_r(jnp.exp(acs[:tq]), expand)

    x = _pad_rows(x_act, T)
    xdt = (x * dt_e).astype(BF16)
    xw = (x * wts_e).astype(BF16)
    incl_q = incl[:tq]
    half = lax.broadcasted_iota(jnp.int32, (tq, LANES), 1) < SSM_HEAD_DIM
    gw = SSM_HPG * SSM_HEAD_DIM

    y_groups = []
    for g in range(SSM_GROUPS):
        bg = _pad_rows(b_act[:, g * D_STATE:(g + 1) * D_STATE], T).astype(BF16)
        cg = c_act[:, g * D_STATE:(g + 1) * D_STATE].astype(BF16)
        cb = _dot_nt(cg, bg)
        hg = h_ref[0, g * gw:(g + 1) * gw, :]
        y_state = _dot_nt(cg, hg)
        pieces = []
        for j in range(SSM_HPG // 2):
            ys = []
            for hh in (2 * j, 2 * j + 1):
                h = g * SSM_HPG + hh
                seg = acs[:tq, h:h + 1] - acst[h:h + 1, :]
                w = cb * jnp.exp(jnp.where(incl_q, seg, -jnp.inf))
                lo = g * gw + j * LANES
                ys.append(jnp.dot(w.astype(BF16), xdt[:, lo:lo + LANES], preferred_element_type=F32))
            pieces.append(jnp.where(half, ys[0], ys[1]))
        y_intra = jnp.concatenate(pieces, axis=1)
        y_groups.append(y_intra + eacs_e[:, g * gw:(g + 1) * gw] * y_state)
        upd = _dot_tn(xw[:, g * gw:(g + 1) * gw], bg)
        for hh in range(SSM_HPG):
            h = g * SSM_HPG + hh
            r0 = g * gw + hh * SSM_HEAD_DIM
            h_ref[0, r0:r0 + SSM_HEAD_DIM, :] = (e_last[:, h:h + 1] * hg[hh * SSM_HEAD_DIM:(hh + 1) * SSM_HEAD_DIM, :]
                                                 + upd[hh * SSM_HEAD_DIM:(hh + 1) * SSM_HEAD_DIM, :])

    y = jnp.concatenate(y_groups, axis=1)
    y = y + dexp_ref[...] * x[:tq]
    y = y * _silu(z_ref[...])
    outs = []
    for g in range(SSM_GROUPS):
        yg = y[:, g * gw:(g + 1) * gw]
        ms = jnp.mean(yg * yg, axis=-1, keepdims=True)
        outs.append(yg * lax.rsqrt(ms + EPS))
    y_ref[...] = (jnp.concatenate(outs, axis=1) * norm_ref[...]).astype(y_ref.dtype)


def _proj_block(tq, nc, width, offset):
    return pl.BlockSpec((tq, width), lambda b, c: (b * nc + c, offset // width))


def _full_block(shape):
    return pl.BlockSpec(shape, lambda b, c: tuple(0 for _ in shape))


def ssd_scan(proj, p, h0, *, nb, nc, tq, valid):
    rows = nb * nc * tq
    cw, cb = p['ssm_conv_w'], p['ssm_conv_b'].reshape(1, SSM_CONV_DIM)
    xs, bs, cs = slice(0, D_SSM), slice(D_SSM, D_SSM + SSM_GN), slice(D_SSM + SSM_GN, SSM_CONV_DIM)
    state = pl.BlockSpec((1, D_SSM, D_STATE), lambda b, c: (b, 0, 0))
    return pl.pallas_call(
        functools.partial(_ssd_kernel, tq=tq, valid=valid),
        grid=(nb, nc),
        in_specs=[_proj_block(tq, nc, D_SSM, P_Z), _proj_block(tq, nc, D_SSM, P_X), _proj_block(tq, nc, SSM_GN, P_B),
                  _proj_block(tq, nc, SSM_GN, P_C), _proj_block(tq, nc, LANES, P_SMALL),
                  _full_block((CONV_K, D_SSM)), _full_block((CONV_K, SSM_GN)), _full_block((CONV_K, SSM_GN)),
                  _full_block((1, D_SSM)), _full_block((1, SSM_GN)), _full_block((1, SSM_GN)),
                  _full_block((1, SSM_HEADS)), _full_block((SSM_HEADS, 1)),
                  _full_block((1, SSM_HEADS)), _full_block((SSM_HEADS, 1)), _full_block((1, D_SSM)), _full_block((1, D_SSM)),
                  state],
        out_specs=[pl.BlockSpec((tq, D_SSM), lambda b, c: (b * nc + c, 0)), state],
        out_shape=[jax.ShapeDtypeStruct((rows, D_SSM), BF16), jax.ShapeDtypeStruct((nb, D_SSM, D_STATE), F32)],
        scratch_shapes=[pltpu.VMEM((tq + CONV_HIST, D_SSM), F32), pltpu.VMEM((tq + CONV_HIST, SSM_GN), F32),
                        pltpu.VMEM((tq + CONV_HIST, SSM_GN), F32)],
        compiler_params=_cparams(("arbitrary", "arbitrary")),
        name="ssd_scan",
    )(proj, proj, proj, proj, proj, cw[:, xs], cw[:, bs], cw[:, cs], cb[:, xs], cb[:, bs], cb[:, cs],
      p['ssm_dt_bias'].reshape(1, SSM_HEADS), p['ssm_dt_bias'].reshape(SSM_HEADS, 1),
      p['ssm_a_log'].reshape(1, SSM_HEADS), p['ssm_a_log'].reshape(SSM_HEADS, 1),
      jnp.repeat(p['ssm_d'], SSM_HEAD_DIM).reshape(1, D_SSM), p['ssm_norm'].reshape(1, D_SSM), h0)


GDN_PAIRS = GDN_HEADS // 2
GDN_PAIR_BATCH = 4
SOLVE_BLOCK = 8


def _bd(w):
    w = w.astype(BF16)
    half = w.shape[1] // 2
    z = jnp.zeros((w.shape[0], half), BF16)
    return jnp.concatenate([jnp.concatenate([w[:, :half], z], axis=1),
                            jnp.concatenate([z, w[:, half:]], axis=1)], axis=0)


def _wdot(x, y_wide):
    return jnp.dot(x.astype(BF16), _bd(y_wide), preferred_element_type=F32)


def _wdot_nt(x, y_wide):
    return lax.dot_general(x.astype(BF16), _bd(y_wide), (((1,), (1,)), ((), ())), preferred_element_type=F32)


def _blocked_unit_lower_inverse(a_list, eye_w, same_block):
    T = eye_w.shape[0]
    n = len(a_list)
    m0 = same_block(SOLVE_BLOCK)
    a_d = [jnp.where(m0, a, 0.0) for a in a_list]
    p = [eye_w - a for a in a_d]
    n_sq = _n_squarings(SOLVE_BLOCK)
    if n_sq > 0:
        pw = [_wdot(a, a) for a in a_d]
        for i in range(n_sq):
            if i + 1 < n_sq:
                r = [_wdot(jnp.concatenate([p[j], pw[j]], axis=0), pw[j]) for j in range(n)]
                p = [p[j] + r[j][:T] for j in range(n)]
                pw = [r[j][T:] for j in range(n)]
            else:
                p = [p[j] + _wdot(p[j], pw[j]) for j in range(n)]
    b = SOLVE_BLOCK
    while b < T:
        m_off = same_block(2 * b) & jnp.logical_not(same_block(b))
        t = [_wdot(p[j], jnp.where(m_off, a_list[j], 0.0)) for j in range(n)]
        p = [p[j] - _wdot(t[j], p[j]) for j in range(n)]
        b *= 2
    return p


def _n_squarings(n_valid):
    n_sq = 0
    while (1 << (n_sq + 1)) < n_valid:
        n_sq += 1
    return n_sq


def _gdn_kernel(qraw_ref, kraw_ref, vraw_ref, gate_ref, small_ref, wq_ref, wk_ref, wv_ref, alog_ref, alogt_ref,
                dtb_ref, dtbt_ref, norm_ref, s0_ref, o_ref, s_ref, bufq, bufk, bufv, q_s, k_s, v_s):
    T = T_PAD
    K = GDN_HEAD_DIM
    W = 2 * K

    @pl.when(pl.program_id(1) == 0)
    def _():
        s_ref[...] = s0_ref[...]

    row = lax.broadcasted_iota(jnp.int32, (T, T), 0)
    col = lax.broadcasted_iota(jnp.int32, (T, T), 1)
    tri_l = (row >= col).astype(F32)
    tri_u = (row <= col).astype(F32)
    row_w = lax.broadcasted_iota(jnp.int32, (T, W), 0)
    lane_w = lax.broadcasted_iota(jnp.int32, (T, W), 1)
    col_w = lane_w % K
    first = lane_w < K
    incl_w = row_w >= col_w
    strict_w = row_w > col_w
    eye_w = (row_w == col_w).astype(F32)
    same_block = lambda b: (row_w // b) == (col_w // b)

    first_chunk = pl.program_id(1) == 0
    q_s[...] = _silu(_causal_conv(qraw_ref, bufq, wq_ref, T, first_chunk))
    k_s[...] = _silu(_causal_conv(kraw_ref, bufk, wk_ref, T, first_chunk))
    v_s[...] = _silu(_causal_conv(vraw_ref, bufv, wv_ref, T, first_chunk))
    q_ref, k_ref, v_ref = q_s, k_s, v_s
    raw, raw_t = _small_and_transpose(small_ref)
    beta = jax.nn.sigmoid(raw[:, S_BETA:S_BETA + GDN_HEADS])
    g = -jnp.exp(alog_ref[...]) * jax.nn.softplus(raw[:, S_A:S_A + GDN_HEADS] + dtb_ref[...])
    gt = -jnp.exp(alogt_ref[...]) * jax.nn.softplus(raw_t[S_A:S_A + GDN_HEADS, :] + dtbt_ref[...])
    gcs = _dot_sel_l(tri_l, g)
    gcst = _dot_sel_r(gt, tri_u)
    egcs = jnp.exp(gcs)
    last = gcs[T - 1:T, :]
    kdec = jnp.exp(last - gcs)
    cd = jnp.exp(last)
    norm_w = jnp.concatenate([norm_ref[...], norm_ref[...]], axis=1)

    def widen(x, p):
        return jnp.where(first[:x.shape[0]], x[:, 2 * p:2 * p + 1], x[:, 2 * p + 1:2 * p + 2])

    def l2n(x):
        sq = x * x
        sa = jnp.sum(sq[:, :K], axis=-1, keepdims=True)
        sb = jnp.sum(sq[:, K:], axis=-1, keepdims=True)
        return x * lax.rsqrt(jnp.where(first, sa, sb) + EPS)

    for p0 in range(0, GDN_PAIRS, GDN_PAIR_BATCH):
        pairs = list(range(p0, p0 + GDN_PAIR_BATCH))
        qs, ks, decays, a_mats = [], [], [], []
        for p in pairs:
            lanes = slice(p * W, (p + 1) * W)
            qs.append(l2n(q_ref[:, lanes]) * (K ** -0.5))
            kw = l2n(k_ref[:, lanes])
            ks.append(kw)
            gcst_w = jnp.concatenate([gcst[2 * p:2 * p + 1, :], gcst[2 * p + 1:2 * p + 2, :]], axis=1)
            decay = jnp.exp(jnp.where(incl_w, widen(gcs, p) - gcst_w, -jnp.inf))
            decays.append(decay)
            a_mats.append(jnp.where(strict_w, widen(beta, p) * _wdot_nt(kw, kw) * decay, 0.0))
        minv = _blocked_unit_lower_inverse(a_mats, eye_w, same_block)
        us = [_wdot(minv[j], v_ref[:, p * W:(p + 1) * W] * widen(beta, p)) for j, p in enumerate(pairs)]
        ws = [_wdot(minv[j], ks[j] * (widen(beta, p) * widen(egcs, p))) for j, p in enumerate(pairs)]
        states = [jnp.concatenate([s_ref[0, 2 * p], s_ref[0, 2 * p + 1]], axis=1) for p in pairs]
        ws_qs = [_wdot(jnp.concatenate([ws[j], qs[j] * widen(egcs, p)], axis=0), states[j])
                 for j, p in enumerate(pairs)]
        v_new = [us[j] - ws_qs[j][:T] for j in range(len(pairs))]
        qk = [_wdot_nt(qs[j], ks[j]) * decays[j] for j in range(len(pairs))]
        for j, p in enumerate(pairs):
            o = ws_qs[j][T:] + _wdot(qk[j], v_new[j])
            kd = ks[j] * widen(kdec, p)
            kd_rows = jnp.concatenate([kd[:, :K], kd[:, K:]], axis=0).astype(BF16)
            upd = lax.dot_general(kd_rows, _bd(v_new[j]), (((0,), (0,)), ((), ())), preferred_element_type=F32)
            s_new = widen(cd, p)[:1] * states[j] + upd
            s_ref[0, 2 * p] = s_new[:, :K]
            s_ref[0, 2 * p + 1] = s_new[:, K:]
            sq = o * o
            ms = jnp.where(first, jnp.mean(sq[:, :K], axis=-1, keepdims=True), jnp.mean(sq[:, K:], axis=-1, keepdims=True))
            lanes = slice(p * W, (p + 1) * W)
            o = o * lax.rsqrt(ms + EPS) * norm_w * _silu(gate_ref[:, lanes])
            o_ref[:, lanes] = o.astype(o_ref.dtype)


def _gdn_params(p):
    cw = p['gdn_conv_w']
    return (cw[:, :D_GDN], cw[:, D_GDN:2 * D_GDN], cw[:, 2 * D_GDN:],
            p['gdn_a_log'].reshape(1, GDN_HEADS), p['gdn_a_log'].reshape(GDN_HEADS, 1),
            p['gdn_dt_bias'].reshape(1, GDN_HEADS), p['gdn_dt_bias'].reshape(GDN_HEADS, 1),
            p['gdn_norm'].reshape(1, GDN_HEAD_DIM))


def _gdn_specs(tq, nc):
    return [_proj_block(tq, nc, D_GDN, P_Q), _proj_block(tq, nc, D_GDN, P_K), _proj_block(tq, nc, D_GDN, P_V),
            _proj_block(tq, nc, D_GDN, P_GATE), _proj_block(tq, nc, LANES, P_SMALL),
            _full_block((CONV_K, D_GDN)), _full_block((CONV_K, D_GDN)), _full_block((CONV_K, D_GDN)),
            _full_block((1, GDN_HEADS)), _full_block((GDN_HEADS, 1)), _full_block((1, GDN_HEADS)), _full_block((GDN_HEADS, 1)),
            _full_block((1, GDN_HEAD_DIM)),
            pl.BlockSpec((1, GDN_HEADS, GDN_HEAD_DIM, GDN_HEAD_DIM), lambda b, c: (b, 0, 0, 0))]


def gdn_scan(proj, p, s0, *, nb, nc):
    tq = T_PAD
    rows = nb * nc * tq
    state = pl.BlockSpec((1, GDN_HEADS, GDN_HEAD_DIM, GDN_HEAD_DIM), lambda b, c: (b, 0, 0, 0))
    conv_buf = pltpu.VMEM((tq + CONV_HIST, D_GDN), F32)
    act = pltpu.VMEM((tq, D_GDN), F32)
    return pl.pallas_call(
        _gdn_kernel,
        grid=(nb, nc),
        in_specs=_gdn_specs(tq, nc),
        out_specs=[pl.BlockSpec((tq, D_GDN), lambda b, c: (b * nc + c, 0)), state],
        out_shape=[jax.ShapeDtypeStruct((rows, D_GDN), BF16),
                   jax.ShapeDtypeStruct((nb, GDN_HEADS, GDN_HEAD_DIM, GDN_HEAD_DIM), F32)],
        scratch_shapes=[conv_buf, conv_buf, conv_buf, act, act, act],
        compiler_params=_cparams(("arbitrary", "arbitrary")),
        name="gdn_scan",
    )(proj, proj, proj, proj, proj, *_gdn_params(p), s0)


def _gdn_decode_kernel(qraw_ref, kraw_ref, vraw_ref, gate_ref, small_ref, wq_ref, wk_ref, wv_ref, alog_ref, alogt_ref,
                       dtb_ref, dtbt_ref, norm_ref, s0_ref, o_ref, s_ref, bufq, bufk, bufv, q_s, k_s, v_s, *, valid):
    R = DEC_ROWS
    K = GDN_HEAD_DIM
    N = GDN_HEADS * R

    def stack(ref):
        return jnp.concatenate([ref[:, h * K:(h + 1) * K] for h in range(GDN_HEADS)], axis=0)

    row = lax.broadcasted_iota(jnp.int32, (N, N), 0)
    col = lax.broadcasted_iota(jnp.int32, (N, N), 1)
    same = (row // R) == (col // R)
    incl = same & (row >= col)
    strict = same & (row > col)
    eye = row == col
    eye_f = eye.astype(F32)

    def to_col(rowvec):
        return jnp.sum(jnp.where(eye, rowvec, 0.0), axis=1, keepdims=True)

    first_chunk = pl.program_id(1) == 0
    q_s[...] = _silu(_causal_conv(qraw_ref, bufq, wq_ref, R, first_chunk))
    k_s[...] = _silu(_causal_conv(kraw_ref, bufk, wk_ref, R, first_chunk))
    v_s[...] = _silu(_causal_conv(vraw_ref, bufv, wv_ref, R, first_chunk))
    q_ref, k_ref, v_ref = q_s, k_s, v_s
    _, raw_t = _small_and_transpose(small_ref)
    _, cmask = _valid_masks(valid, GDN_HEADS)
    beta_t = jnp.where(cmask, jax.nn.sigmoid(raw_t[S_BETA:S_BETA + GDN_HEADS, :]), 0.0)
    g_t = jnp.where(cmask, -jnp.exp(alogt_ref[...]) * jax.nn.softplus(raw_t[S_A:S_A + GDN_HEADS, :] + dtbt_ref[...]), 0.0)
    t_row = lax.broadcasted_iota(jnp.int32, (T_PAD, N), 0)
    t_col = lax.broadcasted_iota(jnp.int32, (T_PAD, N), 1)
    tile = (t_row == t_col % R).astype(F32)
    own = lax.broadcasted_iota(jnp.int32, (GDN_HEADS, N), 0) == lax.broadcasted_iota(jnp.int32, (GDN_HEADS, N), 1) // R
    flat = lambda m: jnp.sum(jnp.where(own, _dot_sel_r(m, tile), 0.0), axis=0, keepdims=True)
    g_row = flat(g_t)
    gb = jnp.concatenate([g_row, jnp.zeros((7, N), F32)], axis=0)
    csum = _dot_sel_r(gb, (same & (row <= col)).astype(F32))
    tot = _dot_sel_r(gb, same.astype(F32))
    gcs_row = csum[0:1, :]
    last_row = tot[0:1, :]
    gcs_col = to_col(gcs_row)
    bcol = to_col(flat(beta_t))
    egcs_col = jnp.exp(gcs_col)
    kdec_col = jnp.exp(to_col(last_row) - gcs_col)
    cd_row = jnp.exp(last_row)

    q_all = stack(q_ref)
    k_all = stack(k_ref)
    v_all = stack(v_ref)
    q_all = q_all * lax.rsqrt(jnp.sum(q_all * q_all, axis=-1, keepdims=True) + EPS) * (K ** -0.5)
    k_all = k_all * lax.rsqrt(jnp.sum(k_all * k_all, axis=-1, keepdims=True) + EPS)

    decay = jnp.exp(jnp.where(incl, gcs_col - gcs_row, -jnp.inf))
    a_mat = jnp.where(strict, bcol * _dot_nt(k_all, k_all) * decay, 0.0)
    minv = eye_f - a_mat
    n_sq = _n_squarings(valid[1] - valid[0])
    pw = a_mat
    for _ in range(n_sq):
        pw = _dot(pw, pw)
        minv = minv + _dot(minv, pw)
    u_all = _dot(minv, v_all * bcol)
    w_all = _dot(minv, k_all * (bcol * egcs_col))
    qd_all = q_all * egcs_col
    qk = _dot_nt(q_all, k_all) * decay
    kd_t = _dot_nt(eye_f[:K, :K], k_all * kdec_col).astype(BF16)

    first = lax.broadcasted_iota(jnp.int32, (K, 2 * K), 1) < K
    states, v_new, q_s = [], [], []
    for p in range(GDN_PAIRS):
        ra = slice(2 * p * R, (2 * p + 1) * R)
        rb = slice((2 * p + 1) * R, (2 * p + 2) * R)
        s_w = jnp.concatenate([s0_ref[0, 2 * p], s0_ref[0, 2 * p + 1]], axis=1)
        states.append(s_w)
        lhs = jnp.concatenate([jnp.concatenate([w_all[ra], w_all[rb]], axis=1),
                               jnp.concatenate([qd_all[ra], qd_all[rb]], axis=1)], axis=0)
        both = _wdot(lhs, s_w)
        v_new += [u_all[ra] - both[:R, :K], u_all[rb] - both[:R, K:]]
        q_s += [both[R:, :K], both[R:, K:]]
    v_new_all = jnp.concatenate(v_new, axis=0)
    o = jnp.concatenate(q_s, axis=0) + _dot(qk, v_new_all)
    ms = jnp.mean(o * o, axis=-1, keepdims=True)
    o = o * lax.rsqrt(ms + EPS) * norm_ref[...] * _silu(stack(gate_ref))
    for h in range(GDN_HEADS):
        o_ref[:, h * K:(h + 1) * K] = o[h * R:(h + 1) * R].astype(o_ref.dtype)
    zero = jnp.zeros((R, 2 * K), BF16)
    zhalf = jnp.zeros((R, K), BF16)
    for p in range(GDN_PAIRS):
        va = jnp.concatenate([v_new[2 * p].astype(BF16), zhalf], axis=1)
        vb = jnp.concatenate([zhalf, v_new[2 * p + 1].astype(BF16)], axis=1)
        rhs = jnp.concatenate([zero] * (2 * p) + [va, vb] + [zero] * (GDN_HEADS - 2 - 2 * p), axis=0)
        upd = jnp.dot(kd_t, rhs, preferred_element_type=F32)
        cd_w = jnp.where(first, cd_row[:, 2 * p * R:2 * p * R + 1], cd_row[:, (2 * p + 1) * R:(2 * p + 1) * R + 1])
        s_new = cd_w * states[p] + upd
        s_ref[0, 2 * p] = s_new[:, :K]
        s_ref[0, 2 * p + 1] = s_new[:, K:]


def gdn_decode(proj, p, s0, *, nb, valid):
    tq = DEC_ROWS
    rows = nb * tq
    state = pl.BlockSpec((1, GDN_HEADS, GDN_HEAD_DIM, GDN_HEAD_DIM), lambda b, c: (b, 0, 0, 0))
    conv_buf = pltpu.VMEM((tq + CONV_HIST, D_GDN), F32)
    act = pltpu.VMEM((tq, D_GDN), F32)
    return pl.pallas_call(
        functools.partial(_gdn_decode_kernel, valid=valid),
        grid=(nb, 1),
        in_specs=_gdn_specs(tq, 1),
        out_specs=[pl.BlockSpec((tq, D_GDN), lambda b, c: (b, 0)), state],
        out_shape=[jax.ShapeDtypeStruct((rows, D_GDN), BF16),
                   jax.ShapeDtypeStruct((nb, GDN_HEADS, GDN_HEAD_DIM, GDN_HEAD_DIM), F32)],
        scratch_shapes=[conv_buf, conv_buf, conv_buf, act, act, act],
        compiler_params=_cparams(("arbitrary", "arbitrary")),
        name="gdn_decode",
    )(proj, proj, proj, proj, proj, *_gdn_params(p), s0)


DEC_TOKEN_ROW = 8


def _mixers(proj, bsz, seq, st_ssm, st_gdn, p):
    if seq >= CHUNK:
        nc, tq, valid = seq // CHUNK, T_PAD, (0, T_PAD)
    else:
        nc, tq, valid = 1, DEC_ROWS, (DEC_TOKEN_ROW, DEC_TOKEN_ROW + seq)
    y, ssm_new = ssd_scan(proj, p, st_ssm.reshape(bsz, D_SSM, D_STATE), nb=bsz, nc=nc, tq=tq, valid=valid)
    if seq >= CHUNK:
        o, gdn_new = gdn_scan(proj, p, st_gdn, nb=bsz, nc=nc)
    else:
        o, gdn_new = gdn_decode(proj, p, st_gdn, nb=bsz, valid=valid)
    return y, o, ssm_new.reshape(bsz, SSM_HEADS, SSM_HEAD_DIM, D_STATE), gdn_new


def _conv_cols_to_ref_order(rows):
    ssm = jnp.concatenate([rows[..., P_X:P_X + D_SSM], rows[..., P_B:P_B + SSM_GN], rows[..., P_C:P_C + SSM_GN]], axis=-1)
    return ssm, rows[..., P_Q:P_Q + GDN_CONV_DIM]


def _decode_blocks(proj_s, bsz, seq, st_ssm_conv, st_gdn_conv):
    hist = jnp.zeros((bsz, CONV_K - 1, P_COLS), F32)
    hist = hist.at[..., P_X:P_X + D_SSM].set(st_ssm_conv[..., :D_SSM])
    hist = hist.at[..., P_B:P_B + SSM_GN].set(st_ssm_conv[..., D_SSM:D_SSM + SSM_GN])
    hist = hist.at[..., P_C:P_C + SSM_GN].set(st_ssm_conv[..., D_SSM + SSM_GN:])
    hist = hist.at[..., P_Q:P_Q + GDN_CONV_DIM].set(st_gdn_conv)
    lead = jnp.zeros((bsz, DEC_TOKEN_ROW - (CONV_K - 1), P_COLS), F32)
    trail = jnp.zeros((bsz, DEC_ROWS - DEC_TOKEN_ROW - seq, P_COLS), F32)
    blocks = jnp.concatenate([lead, hist, proj_s.reshape(bsz, seq, P_COLS), trail], axis=1)
    return blocks.reshape(bsz * DEC_ROWS, P_COLS)


def _reorder_w_in(w):
    small = jnp.concatenate([w[:, SPLIT_XBC:SPLIT_DT], w[:, SPLIT_GATE:]], axis=1)
    small = jnp.pad(small, ((0, 0), (0, LANES - small.shape[1])))
    xbc = SPLIT_Z
    return jnp.concatenate([w[:, :SPLIT_Z], w[:, xbc:xbc + D_SSM], w[:, SPLIT_DT:SPLIT_GATE],
                            w[:, xbc + D_SSM:xbc + D_SSM + 2 * SSM_GN], small], axis=1).astype(BF16)


def _dense_ffn(h, x, wg, wu, wd):
    m = h.shape[0]
    tm, sub = 1088, 272
    nt = m // tm
    te = jnp.zeros((nt,), jnp.int32)
    tmap = jnp.arange(nt, dtype=jnp.int32)
    nv = jnp.full((nt,), tm // sub, jnp.int32)
    hid = ffn_up(h, wg, wu, te, tmap, nv, tm=tm, sub=sub, tn=256)
    y = ffn_down(hid, wd, te, tmap, nv, tm=tm, sub=sub, tn=2048, tk=1024)
    return x + y


def _moe_ffn(h, x, route, wg, wu, wd):
    m = h.shape[0]
    tm, sub = 1024, 256
    nt = (m * TOP_K) // tm + N_EXPERTS
    top_i = route[:, :TOP_K].astype(jnp.int32)
    probs = route[:, TOP_K:2 * TOP_K]
    e_flat = top_i.reshape(-1)
    n_assign = e_flat.shape[0]
    onehot = (e_flat[:, None] == jnp.arange(N_EXPERTS, dtype=jnp.int32)[None, :]).astype(jnp.int32)
    csum = jnp.cumsum(onehot, axis=0)
    counts = csum[-1]
    rank = jnp.sum(onehot * csum, axis=1) - 1
    tiles_per_e = (counts + tm - 1) // tm
    tile_end_e = jnp.cumsum(tiles_per_e)
    tile_start_e = tile_end_e - tiles_per_e
    pos = jnp.sum(onehot * tile_start_e[None, :], axis=1) * tm + rank
    src = jnp.zeros((nt * tm,), jnp.int32).at[pos].set(jnp.arange(n_assign, dtype=jnp.int32) // TOP_K)
    n_used = tile_end_e[-1]
    tile_ids = jnp.arange(nt, dtype=jnp.int32)
    tmap = jnp.minimum(tile_ids, n_used - 1)
    te = jnp.minimum(jnp.sum((tile_end_e[None, :] <= tmap[:, None]).astype(jnp.int32), axis=1), N_EXPERTS - 1)
    rows_in_tile = jnp.clip(counts[te] - (tmap - tile_start_e[te]) * tm, 0, tm)
    nv = jnp.where(tile_ids < n_used, (rows_in_tile + sub - 1) // sub, 0).astype(jnp.int32)

    xs = jnp.take(h, src, axis=0)
    hid = ffn_up(xs, wg, wu, te, tmap, nv, tm=tm, sub=sub, tn=256)
    ys = ffn_down(hid, wd, te, tmap, nv, tm=tm, sub=sub, tn=2048, tk=1024)
    pos = pos.reshape(m, TOP_K)
    out = probs[:, 0:1] * jnp.take(ys, pos[:, 0], axis=0)
    out = out + probs[:, 1:2] * jnp.take(ys, pos[:, 1], axis=0)
    return x + out


def kernel(x_prompt, x_sample, state_ssm, state_ssm_conv, state_gdn, state_gdn_conv, norm_mix, w_in, ssm_conv_w, ssm_conv_b, ssm_dt_bias, ssm_a_log, ssm_d, ssm_norm, gdn_conv_w, gdn_dt_bias, gdn_a_log, gdn_norm, w_out, norm_ffn, dense_w_gate, dense_w_up, dense_w_down, moe_router, moe_w_gate, moe_w_up, moe_w_down, norm_final):
    bp, lp, d = x_prompt.shape
    bs, ls, _ = x_sample.shape
    mp, ms = bp * lp, bs * ls
    x = jnp.concatenate([x_prompt.reshape(mp, d), x_sample.reshape(ms, d)], axis=0)
    m = mp + ms
    tm_all = m // 8

    new_p = [[], [], [], []]
    new_s = [[], [], [], []]
    for i in range(DEPTH):
        p = dict(ssm_conv_w=ssm_conv_w[i], ssm_conv_b=ssm_conv_b[i], ssm_dt_bias=ssm_dt_bias[i],
                 ssm_a_log=ssm_a_log[i], ssm_d=ssm_d[i], ssm_norm=ssm_norm[i], gdn_conv_w=gdn_conv_w[i],
                 gdn_dt_bias=gdn_dt_bias[i], gdn_a_log=gdn_a_log[i], gdn_norm=gdn_norm[i])
        h = rmsnorm(x, norm_mix[i], BF16)
        proj = matmul(h, _reorder_w_in(w_in[i]), tm=tm_all, tn=640, sub=tm_all // 4)
        zero_ssm = jnp.zeros((bp, SSM_HEADS, SSM_HEAD_DIM, D_STATE), F32)
        zero_gdn = jnp.zeros((bp, GDN_HEADS, GDN_HEAD_DIM, GDN_HEAD_DIM), F32)
        y_p, o_p, ssm_p, gdn_p = _mixers(proj, bp, lp, zero_ssm, zero_gdn, p)
        proj_s = proj[mp:]
        y_s, o_s, ssm_s, gdn_s = _mixers(_decode_blocks(proj_s, bs, ls, state_ssm_conv[i], state_gdn_conv[i]),
                                         bs, ls, state_ssm[i], state_gdn[i], p)
        tok = lambda t: t.reshape(bs, DEC_ROWS, -1)[:, DEC_TOKEN_ROW:DEC_TOKEN_ROW + ls].reshape(ms, -1)
        conv_p = _conv_cols_to_ref_order(proj[:mp].reshape(bp, lp, P_COLS)[:, lp - (CONV_K - 1):])
        ssm_tail, gdn_tail = _conv_cols_to_ref_order(proj_s.reshape(bs, ls, P_COLS))
        conv_s = (jnp.concatenate([state_ssm_conv[i], ssm_tail], axis=1)[:, ls:],
                  jnp.concatenate([state_gdn_conv[i], gdn_tail], axis=1)[:, ls:])
        for k, (vp, vs) in enumerate(((ssm_p, ssm_s), (conv_p[0], conv_s[0]), (gdn_p, gdn_s), (conv_p[1], conv_s[1]))):
            new_p[k].append(vp)
            new_s[k].append(vs)
        mixed = jnp.concatenate([jnp.concatenate([y_p[:mp], o_p[:mp]], axis=1),
                                 jnp.concatenate([tok(y_s), tok(o_s)], axis=1)], axis=0)
        x = matmul(mixed, w_out[i].astype(BF16), x, tm=tm_all, tn=512, sub=tm_all // 4)
        j = i // 2
        if i % 2 == 0:
            h = rmsnorm(x, norm_ffn[i], BF16)
            x = _dense_ffn(h, x, dense_w_gate[j:j + 1], dense_w_up[j:j + 1], dense_w_down[j:j + 1])
        else:
            h, route = rmsnorm_router(x, norm_ffn[i], moe_router[j])
            x = _moe_ffn(h, x, route, moe_w_gate[j], moe_w_up[j], moe_w_down[j])
    y = rmsnorm(x, norm_final, F32)
    return (y[:mp].reshape(bp, lp, d), y[mp:].reshape(bs, ls, d),
            jnp.stack(new_p[0]), jnp.stack(new_p[1]), jnp.stack(new_p[2]), jnp.stack(new_p[3]),
            jnp.stack(new_s[0]), jnp.stack(new_s[1]), jnp.stack(new_s[2]), jnp.stack(new_s[3]))
```

```python
import functools

import jax
import jax.numpy as jnp
from jax import lax
from jax.experimental import pallas as pl
from jax.experimental.pallas import tpu as pltpu

F32 = jnp.float32
BF16 = jnp.bfloat16

D_MODEL = 4096
DEPTH = 2
D_SSM = 2048
D_GDN = 2048
SSM_HEAD_DIM = 64
SSM_HEADS = 32
SSM_GROUPS = 4
SSM_HPG = 8
D_STATE = 128
SSM_GN = SSM_GROUPS * D_STATE
SSM_CONV_DIM = D_SSM + 2 * SSM_GN
GDN_HEAD_DIM = 128
GDN_HEADS = 16
GDN_CONV_DIM = 3 * D_GDN
CONV_K = 4
CHUNK = 128
SPLIT_Z = D_SSM
SPLIT_XBC = SPLIT_Z + SSM_CONV_DIM
SPLIT_DT = SPLIT_XBC + SSM_HEADS
SPLIT_QKV = SPLIT_DT + GDN_CONV_DIM
SPLIT_GATE = SPLIT_QKV + D_GDN
SPLIT_BETA = SPLIT_GATE + GDN_HEADS
IN_COLS = SPLIT_BETA + GDN_HEADS
D_FF = 14336
N_EXPERTS = 8
TOP_K = 2
EPS = 1e-6

P_Z, P_X, P_Q, P_K, P_V, P_GATE = 0, 2048, 4096, 6144, 8192, 10240
P_B, P_C, P_SMALL = 12288, 12800, 13312
P_COLS = 13824
S_DT, S_BETA, S_A = 0, SSM_HEADS, SSM_HEADS + GDN_HEADS
CONV_HIST = 8

LANES = 128
T_PAD = 128
DEC_ROWS = 16
VMEM_LIMIT = 56 * 1024 * 1024


def _cparams(sem):
    return pltpu.CompilerParams(dimension_semantics=sem, vmem_limit_bytes=VMEM_LIMIT)


def _dot(a, b):
    return jnp.dot(a.astype(BF16), b.astype(BF16), preferred_element_type=F32)


def _dot_nt(a, b):
    return lax.dot_general(a.astype(BF16), b.astype(BF16), (((1,), (1,)), ((), ())),
                           preferred_element_type=F32)


def _dot_tn(a, b):
    return lax.dot_general(a.astype(BF16), b.astype(BF16), (((0,), (0,)), ((), ())),
                           preferred_element_type=F32)


def _split3(a):
    a1 = a.astype(BF16)
    r1 = a - a1.astype(F32)
    a2 = r1.astype(BF16)
    a3 = (r1 - a2.astype(F32)).astype(BF16)
    return a1, a2, a3


def _dot_sel_l(sel, a):
    a1, a2, a3 = _split3(a)
    s = sel.astype(BF16)
    d = lambda p: jnp.dot(s, p, preferred_element_type=F32)
    return (d(a3) + d(a2)) + d(a1)


def _dot_sel_r(a, sel):
    a1, a2, a3 = _split3(a)
    s = sel.astype(BF16)
    d = lambda p: jnp.dot(p, s, preferred_element_type=F32)
    return (d(a3) + d(a2)) + d(a1)


def _dot_hi(a, b):
    a1 = a.astype(BF16)
    a2 = (a - a1.astype(F32)).astype(BF16)
    b1 = b.astype(BF16)
    b2 = (b - b1.astype(F32)).astype(BF16)
    d = lambda p, q: jnp.dot(p, q, preferred_element_type=F32)
    return (d(a2, b1) + d(a1, b2)) + d(a1, b1)


def _pad_rows(v, rows):
    if v.shape[0] == rows:
        return v
    return jnp.concatenate([v, jnp.zeros((rows - v.shape[0], v.shape[1]), v.dtype)], axis=0)


def _silu(v):
    return v * jax.nn.sigmoid(v)


def _causal_conv(raw_ref, buf_ref, w_ref, tq, first_chunk):
    @pl.when(first_chunk)
    def _():
        buf_ref[0:CONV_HIST, :] = jnp.zeros((CONV_HIST, buf_ref.shape[1]), F32)

    buf_ref[CONV_HIST:CONV_HIST + tq, :] = raw_ref[...]
    w = w_ref[...]
    lo = CONV_HIST - (CONV_K - 1)
    y = w[0:1] * buf_ref[lo:lo + tq, :]
    for j in range(1, CONV_K):
        y = y + w[j:j + 1] * buf_ref[lo + j:lo + j + tq, :]
    buf_ref[0:CONV_HIST, :] = buf_ref[tq:tq + CONV_HIST, :]
    return y


def _small_and_transpose(small_ref):
    raw = _pad_rows(small_ref[...], T_PAD)
    row = lax.broadcasted_iota(jnp.int32, (T_PAD, T_PAD), 0)
    col = lax.broadcasted_iota(jnp.int32, (T_PAD, T_PAD), 1)
    eye = (row == col).astype(BF16)
    a1, a2, a3 = _split3(raw)
    d = lambda p: lax.dot_general(p, eye, (((0,), (0,)), ((), ())), preferred_element_type=F32)
    return raw, (d(a3) + d(a2)) + d(a1)


def _valid_masks(valid, width):
    lo, hi = valid
    r = lax.broadcasted_iota(jnp.int32, (T_PAD, width), 0)
    c = lax.broadcasted_iota(jnp.int32, (width, T_PAD), 1)
    return (r >= lo) & (r < hi), (c >= lo) & (c < hi)


def _rmsnorm_kernel(x_ref, g_ref, o_ref):
    x = x_ref[...]
    ms = jnp.mean(x * x, axis=-1, keepdims=True)
    o_ref[...] = (x * lax.rsqrt(ms + EPS) * g_ref[...]).astype(o_ref.dtype)


def rmsnorm(x, gain, out_dtype, tm=256):
    m, d = x.shape
    return pl.pallas_call(
        _rmsnorm_kernel,
        grid=(m // tm,),
        in_specs=[pl.BlockSpec((tm, d), lambda i: (i, 0)), pl.BlockSpec((1, d), lambda i: (0, 0))],
        out_specs=pl.BlockSpec((tm, d), lambda i: (i, 0)),
        out_shape=jax.ShapeDtypeStruct((m, d), out_dtype),
        compiler_params=_cparams(("parallel",)),
        name="rmsnorm",
    )(x, gain.reshape(1, d))


def _rmsnorm_router_kernel(x_ref, g_ref, r_ref, o_ref, route_ref):
    x = x_ref[...]
    ms = jnp.mean(x * x, axis=-1, keepdims=True)
    h = x * lax.rsqrt(ms + EPS) * g_ref[...]
    o_ref[...] = h.astype(o_ref.dtype)
    logits = _dot_hi(h, r_ref[...])
    lane = lax.broadcasted_iota(jnp.int32, logits.shape, 1)
    neg = jnp.float32(-jnp.inf)
    lm = jnp.where(lane < N_EXPERTS, logits, neg)
    m1 = jnp.max(lm, axis=-1, keepdims=True)
    i1 = jnp.min(jnp.where(lm == m1, lane, LANES), axis=-1, keepdims=True)
    lm2 = jnp.where(lane == i1, neg, lm)
    m2 = jnp.max(lm2, axis=-1, keepdims=True)
    i2 = jnp.min(jnp.where(lm2 == m2, lane, LANES), axis=-1, keepdims=True)
    e = jnp.exp(m2 - m1)
    p1 = 1.0 / (1.0 + e)
    p2 = e / (1.0 + e)
    route_ref[...] = jnp.where(lane == 0, i1.astype(F32),
                               jnp.where(lane == 1, i2.astype(F32),
                                         jnp.where(lane == 2, p1, jnp.where(lane == 3, p2, 0.0))))


def rmsnorm_router(x, gain, router, tm=256):
    m, d = x.shape
    r_pad = jnp.zeros((d, LANES), F32).at[:, :N_EXPERTS].set(router)
    return pl.pallas_call(
        _rmsnorm_router_kernel,
        grid=(m // tm,),
        in_specs=[pl.BlockSpec((tm, d), lambda i: (i, 0)), pl.BlockSpec((1, d), lambda i: (0, 0)),
                  pl.BlockSpec((d, LANES), lambda i: (0, 0))],
        out_specs=[pl.BlockSpec((tm, d), lambda i: (i, 0)), pl.BlockSpec((tm, LANES), lambda i: (i, 0))],
        out_shape=[jax.ShapeDtypeStruct((m, d), BF16), jax.ShapeDtypeStruct((m, LANES), F32)],
        compiler_params=_cparams(("parallel",)),
        name="rmsnorm_router",
    )(x, gain.reshape(1, d), r_pad)


def _matmul_kernel(*refs, sub, nsub, has_res):
    if has_res:
        x_ref, w_ref, r_ref, o_ref = refs
    else:
        x_ref, w_ref, o_ref = refs
        r_ref = None

    def body(i, carry):
        rows = pl.ds(pl.multiple_of(i * sub, sub), sub)
        acc = jnp.dot(x_ref[rows, :], w_ref[...], preferred_element_type=F32)
        if has_res:
            acc = acc + r_ref[rows, :]
        o_ref[rows, :] = acc.astype(o_ref.dtype)
        return carry

    lax.fori_loop(0, nsub, body, 0)


def matmul(x, w, res=None, *, tm, tn, sub, out_dtype=F32):
    m, k = x.shape
    n = w.shape[1]
    in_specs = [pl.BlockSpec((tm, k), lambda i, j: (i, 0)), pl.BlockSpec((k, tn), lambda i, j: (0, j))]
    args = [x, w]
    if res is not None:
        in_specs.append(pl.BlockSpec((tm, tn), lambda i, j: (i, j)))
        args.append(res)
    return pl.pallas_call(
        functools.partial(_matmul_kernel, sub=sub, nsub=tm // sub, has_res=res is not None),
        grid=(m // tm, n // tn),
        in_specs=in_specs,
        out_specs=pl.BlockSpec((tm, tn), lambda i, j: (i, j)),
        out_shape=jax.ShapeDtypeStruct((m, n), out_dtype),
        compiler_params=_cparams(("parallel", "arbitrary")),
        name="matmul",
    )(*args)


def _out_proj_kernel(yp_ref, op_ref, ys_ref, os_ref, wy_ref, wo_ref, r_ref, o_ref, *, sub, nsub, n_prompt_tiles):
    def run(y_ref, g_ref):
        def body(i, carry):
            rows = pl.ds(pl.multiple_of(i * sub, sub), sub)
            acc = jnp.dot(y_ref[rows, :], wy_ref[...], preferred_element_type=F32)
            acc = acc + jnp.dot(g_ref[rows, :], wo_ref[...], preferred_element_type=F32)
            o_ref[rows, :] = acc + r_ref[rows, :]
            return carry

        lax.fori_loop(0, nsub, body, 0)

    is_prompt = pl.program_id(0) < n_prompt_tiles

    @pl.when(is_prompt)
    def _():
        run(yp_ref, op_ref)

    @pl.when(jnp.logical_not(is_prompt))
    def _():
        run(ys_ref, os_ref)


def out_proj(y_p, o_p, y_s, o_s, w, res, *, tm, tn, sub):
    mp, k = y_p.shape
    m, n = res.shape
    npt = mp // tm
    nt = npt + y_s.shape[0] // tm
    p_blk = pl.BlockSpec((tm, k), lambda i, j: (jnp.minimum(i, npt - 1), 0))
    s_blk = pl.BlockSpec((tm, k), lambda i, j: (jnp.maximum(i - npt, 0), 0))
    return pl.pallas_call(
        functools.partial(_out_proj_kernel, sub=sub, nsub=tm // sub, n_prompt_tiles=npt),
        grid=(nt, n // tn),
        in_specs=[p_blk, p_blk, s_blk, s_blk,
                  pl.BlockSpec((k, tn), lambda i, j: (0, j)), pl.BlockSpec((k, tn), lambda i, j: (1, j)),
                  pl.BlockSpec((tm, tn), lambda i, j: (i, j))],
        out_specs=pl.BlockSpec((tm, tn), lambda i, j: (i, j)),
        out_shape=jax.ShapeDtypeStruct((m, n), F32),
        compiler_params=_cparams(("parallel", "arbitrary")),
        name="out_proj",
    )(y_p, o_p, y_s, o_s, w, w, res)


def _row_blocks(nv, sub, gr, big, small):
    per = sub // gr
    n_big = nv // per

    def big_body(i, carry):
        big(pl.multiple_of(i * sub, sub))
        return carry

    lax.fori_loop(0, n_big, big_body, 0)
    if gr < sub:
        def small_body(i, carry):
            small(pl.multiple_of(i * gr, gr))
            return carry

        lax.fori_loop(n_big * per, nv, small_body, 0)


def _ffn_up_kernel(te_ref, tmap_ref, nv_ref, x_ref, wg_ref, wu_ref, o_ref, *, sub, gr, ngr):
    nv = nv_ref[pl.program_id(0)]

    @pl.when(nv > 0)
    def _():
        def block(size):
            def run(r0):
                rows = pl.ds(r0, size)
                xs = x_ref[rows, :]
                g = jnp.dot(xs, wg_ref[0].astype(BF16), preferred_element_type=F32)
                u = jnp.dot(xs, wu_ref[0].astype(BF16), preferred_element_type=F32)
                o_ref[rows, :] = (_silu(g) * u).astype(o_ref.dtype)
            return run

        _row_blocks(nv, sub, gr, block(sub), block(gr))

        def zero(i, carry):
            rows = pl.ds(pl.multiple_of(i * gr, gr), gr)
            o_ref[rows, :] = jnp.zeros((gr, o_ref.shape[1]), o_ref.dtype)
            return carry

        lax.fori_loop(nv, ngr, zero, 0)


def ffn_up(x, wg, wu, tile_e, tile_map, tile_nv, *, tm, sub, gr, tn):
    rows, d = x.shape
    nt = rows // tm
    f = wg.shape[2]
    nj = f // tn

    def wmap(t, j, te, tmap, nv):
        return (te[t], 0, jnp.where(nv[t] > 0, j, nj - 1))

    return pl.pallas_call(
        functools.partial(_ffn_up_kernel, sub=sub, gr=gr, ngr=tm // gr),
        grid_spec=pltpu.PrefetchScalarGridSpec(
            num_scalar_prefetch=3,
            grid=(nt, nj),
            in_specs=[pl.BlockSpec((tm, d), lambda t, j, te, tmap, nv: (tmap[t], 0), pipeline_mode=pl.Buffered(1)),
                      pl.BlockSpec((1, d, tn), wmap),
                      pl.BlockSpec((1, d, tn), wmap)],
            out_specs=pl.BlockSpec((tm, tn), lambda t, j, te, tmap, nv: (tmap[t], jnp.where(nv[t] > 0, j, nj - 1))),
        ),
        out_shape=jax.ShapeDtypeStruct((rows, f), BF16),
        compiler_params=_cparams(("arbitrary", "arbitrary")),
        name="ffn_up",
    )(tile_e, tile_map, tile_nv, x, wg, wu)


def _ffn_down_kernel(te_ref, tmap_ref, nv_ref, x_ref, w_ref, *rest, sub, gr, has_res):
    if has_res:
        r_ref, o_ref = rest
    else:
        (o_ref,) = rest
    k = pl.program_id(2)
    nv = nv_ref[pl.program_id(0)]

    @pl.when(nv > 0)
    def _():
        @pl.when(k == 0)
        def _():
            o_ref[...] = r_ref[...] if has_res else jnp.zeros(o_ref.shape, o_ref.dtype)

        def block(size):
            def run(r0):
                rows = pl.ds(r0, size)
                o_ref[rows, :] += jnp.dot(x_ref[rows, :], w_ref[0].astype(BF16), preferred_element_type=F32)
            return run

        _row_blocks(nv, sub, gr, block(sub), block(gr))


def ffn_down(x, wd, tile_e, tile_map, tile_nv, res=None, *, tm, sub, gr, tn, tk):
    rows, f = x.shape
    nt = rows // tm
    d = wd.shape[2]
    nj, nk = d // tn, f // tk

    def live(t, idx, last, nv):
        return jnp.where(nv[t] > 0, idx, last)

    omap = lambda t, j, k, te, tmap, nv: (tmap[t], live(t, j, nj - 1, nv))
    in_specs = [pl.BlockSpec((tm, tk), lambda t, j, k, te, tmap, nv: (tmap[t], live(t, k, nk - 1, nv))),
                pl.BlockSpec((1, tk, tn), lambda t, j, k, te, tmap, nv:
                             (te[t], live(t, k, nk - 1, nv), live(t, j, nj - 1, nv)))]
    args = [tile_e, tile_map, tile_nv, x, wd]
    if res is not None:
        in_specs.append(pl.BlockSpec((tm, tn), omap))
        args.append(res)
    return pl.pallas_call(
        functools.partial(_ffn_down_kernel, sub=sub, gr=gr, has_res=res is not None),
        grid_spec=pltpu.PrefetchScalarGridSpec(
            num_scalar_prefetch=3,
            grid=(nt, nj, nk),
            in_specs=in_specs,
            out_specs=pl.BlockSpec((tm, tn), omap),
        ),
        out_shape=jax.ShapeDtypeStruct((rows, d), F32),
        compiler_params=_cparams(("arbitrary", "arbitrary", "arbitrary")),
        name="ffn_down",
    )(*args)


def _ssd_kernel(z_ref, xraw_ref, braw_ref, craw_ref, small_ref, wx_ref, wb_ref, wc_ref, bx_ref, bb_ref, bc_ref,
                dtb_ref, dtbt_ref, alog_ref, alogt_ref, dexp_ref, norm_ref, h0_ref,
                y_ref, h_ref, bufx, bufb, bufc, *, tq, valid):
    T = T_PAD

    @pl.when(pl.program_id(1) == 0)
    def _():
        h_ref[...] = h0_ref[...]

    row = lax.broadcasted_iota(jnp.int32, (T, T), 0)
    col = lax.broadcasted_iota(jnp.int32, (T, T), 1)
    incl = (row >= col)
    tri_l = incl.astype(F32)
    tri_u = (row <= col).astype(F32)

    first_chunk = pl.program_id(1) == 0
    x_act = _silu(_causal_conv(xraw_ref, bufx, wx_ref, tq, first_chunk) + bx_ref[...])
    b_act = _silu(_causal_conv(braw_ref, bufb, wb_ref, tq, first_chunk) + bb_ref[...])
    c_act = _silu(_causal_conv(craw_ref, bufc, wc_ref, tq, first_chunk) + bc_ref[...])
    raw, raw_t = _small_and_transpose(small_ref)
    rmask, cmask = _valid_masks(valid, SSM_HEADS)
    dt = jnp.where(rmask, jax.nn.softplus(raw[:, S_DT:S_DT + SSM_HEADS] + dtb_ref[...]), 0.0)
    dtt = jnp.where(cmask, jax.nn.softplus(raw_t[S_DT:S_DT + SSM_HEADS, :] + dtbt_ref[...]), 0.0)
    a = dt * (-jnp.exp(alog_ref[...]))
    at = dtt * (-jnp.exp(alogt_ref[...]))
    acs = _dot_sel_l(tri_l, a)
    acst = _dot_sel_r(at, tri_u)
    last = acs[T - 1:T, :]
    e_last = jnp.exp(last)
    wts = jnp.exp(last - acs) * dt

    hp = lax.broadcasted_iota(jnp.int32, (SSM_HEADS, D_SSM), 1) // SSM_HEAD_DIM
    expand = (hp == lax.broadcasted_iota(jnp.int32, (SSM_HEADS, D_SSM), 0)).astype(F32)
    dt_e = _dot_sel_r(dt, expand)
    wts_e = _dot_sel_r(wts, expand)
    eacs_e = _dot_sel_r(jnp.exp(acs[:tq]), expand)

    x = _pad_rows(x_act, T)
    xdt = (x * dt_e).astype(BF16)
    xw = (x * wts_e).astype(BF16)
    incl_q = incl[:tq]
    half = lax.broadcasted_iota(jnp.int32, (tq, LANES), 1) < SSM_HEAD_DIM
    gw = SSM_HPG * SSM_HEAD_DIM

    y_groups = []
    for g in range(SSM_GROUPS):
        bg = _pad_rows(b_act[:, g * D_STATE:(g + 1) * D_STATE], T).astype(BF16)
        cg = c_act[:, g * D_STATE:(g + 1) * D_STATE].astype(BF16)
        cb = _dot_nt(cg, bg)
        hg = h_ref[0, g * gw:(g + 1) * gw, :]
        y_state = _dot_nt(cg, hg)
        pieces = []
        for j in range(SSM_HPG // 2):
            ys = []
            for hh in (2 * j, 2 * j + 1):
                h = g * SSM_HPG + hh
                seg = acs[:tq, h:h + 1] - acst[h:h + 1, :]
                w = cb * jnp.exp(jnp.where(incl_q, seg, -jnp.inf))
                lo = g * gw + j * LANES
                ys.append(jnp.dot(w.astype(BF16), xdt[:, lo:lo + LANES], preferred_element_type=F32))
            pieces.append(jnp.where(half, ys[0], ys[1]))
        y_intra = jnp.concatenate(pieces, axis=1)
        y_groups.append(y_intra + eacs_e[:, g * gw:(g + 1) * gw] * y_state)
        upd = _dot_tn(xw[:, g * gw:(g + 1) * gw], bg)
        for hh in range(SSM_HPG):
            h = g * SSM_HPG + hh
            r0 = g * gw + hh * SSM_HEAD_DIM
            h_ref[0, r0:r0 + SSM_HEAD_DIM, :] = (e_last[:, h:h + 1] * hg[hh * SSM_HEAD_DIM:(hh + 1) * SSM_HEAD_DIM, :]
                                                 + upd[hh * SSM_HEAD_DIM:(hh + 1) * SSM_HEAD_DIM, :])

    y = jnp.concatenate(y_groups, axis=1)
    y = y + dexp_ref[...] * x[:tq]
    y = y * _silu(z_ref[...])
    outs = []
    for g in range(SSM_GROUPS):
        yg = y[:, g * gw:(g + 1) * gw]
        ms = jnp.mean(yg * yg, axis=-1, keepdims=True)
        outs.append(yg * lax.rsqrt(ms + EPS))
    y_ref[...] = (jnp.concatenate(outs, axis=1) * norm_ref[...]).astype(y_ref.dtype)


def _proj_block(tq, nc, width, offset):
    return pl.BlockSpec((tq, width), lambda b, c: (b * nc + c, offset // width))


def _full_block(shape):
    return pl.BlockSpec(shape, lambda b, c: tuple(0 for _ in shape))


def ssd_scan(proj, p, h0, layer, *, nb, nc, tq, valid):
    rows = nb * nc * tq
    cw, cb = p['ssm_conv_w'], p['ssm_conv_b'].reshape(1, SSM_CONV_DIM)
    xs, bs, cs = slice(0, D_SSM), slice(D_SSM, D_SSM + SSM_GN), slice(D_SSM + SSM_GN, SSM_CONV_DIM)
    state = pl.BlockSpec((1, D_SSM, D_STATE), lambda b, c: (b, 0, 0))
    state_in = pl.BlockSpec((None, 1, D_SSM, D_STATE), lambda b, c: (layer, b, 0, 0))
    return pl.pallas_call(
        functools.partial(_ssd_kernel, tq=tq, valid=valid),
        grid=(nb, nc),
        in_specs=[_proj_block(tq, nc, D_SSM, P_Z), _proj_block(tq, nc, D_SSM, P_X), _proj_block(tq, nc, SSM_GN, P_B),
                  _proj_block(tq, nc, SSM_GN, P_C), _proj_block(tq, nc, LANES, P_SMALL),
                  _full_block((CONV_K, D_SSM)), _full_block((CONV_K, SSM_GN)), _full_block((CONV_K, SSM_GN)),
                  _full_block((1, D_SSM)), _full_block((1, SSM_GN)), _full_block((1, SSM_GN)),
                  _full_block((1, SSM_HEADS)), _full_block((SSM_HEADS, 1)),
                  _full_block((1, SSM_HEADS)), _full_block((SSM_HEADS, 1)), _full_block((1, D_SSM)), _full_block((1, D_SSM)),
                  state_in],
        out_specs=[pl.BlockSpec((tq, D_SSM), lambda b, c: (b * nc + c, 0)), state],
        out_shape=[jax.ShapeDtypeStruct((rows, D_SSM), BF16), jax.ShapeDtypeStruct((nb, D_SSM, D_STATE), F32)],
        scratch_shapes=[pltpu.VMEM((tq + CONV_HIST, D_SSM), F32), pltpu.VMEM((tq + CONV_HIST, SSM_GN), F32),
                        pltpu.VMEM((tq + CONV_HIST, SSM_GN), F32)],
        compiler_params=_cparams(("arbitrary", "arbitrary")),
        name="ssd_scan",
    )(proj, proj, proj, proj, proj, cw[:, xs], cw[:, bs], cw[:, cs], cb[:, xs], cb[:, bs], cb[:, cs],
      p['ssm_dt_bias'].reshape(1, SSM_HEADS), p['ssm_dt_bias'].reshape(SSM_HEADS, 1),
      p['ssm_a_log'].reshape(1, SSM_HEADS), p['ssm_a_log'].reshape(SSM_HEADS, 1),
      jnp.repeat(p['ssm_d'], SSM_HEAD_DIM).reshape(1, D_SSM), p['ssm_norm'].reshape(1, D_SSM), h0)


GDN_PAIRS = GDN_HEADS // 2
GDN_PAIR_BATCH = 4
SOLVE_BLOCK = 8


def _bd(w):
    w = w.astype(BF16)
    half = w.shape[1] // 2
    z = jnp.zeros((w.shape[0], half), BF16)
    return jnp.concatenate([jnp.concatenate([w[:, :half], z], axis=1),
                            jnp.concatenate([z, w[:, half:]], axis=1)], axis=0)


def _wdot(x, y_wide):
    return jnp.dot(x.astype(BF16), _bd(y_wide), preferred_element_type=F32)


def _wdot_nt(x, y_wide):
    return lax.dot_general(x.astype(BF16), _bd(y_wide), (((1,), (1,)), ((), ())), preferred_element_type=F32)


def _blocked_unit_lower_inverse(a_list, eye_w, same_block):
    T = eye_w.shape[0]
    n = len(a_list)
    m0 = same_block(SOLVE_BLOCK)
    a_d = [jnp.where(m0, a, 0.0) for a in a_list]
    p = [eye_w - a for a in a_d]
    n_sq = _n_squarings(SOLVE_BLOCK)
    if n_sq > 0:
        pw = [_wdot(a, a) for a in a_d]
        for i in range(n_sq):
            if i + 1 < n_sq:
                r = [_wdot(jnp.concatenate([p[j], pw[j]], axis=0), pw[j]) for j in range(n)]
                p = [p[j] + r[j][:T] for j in range(n)]
                pw = [r[j][T:] for j in range(n)]
            else:
                p = [p[j] + _wdot(p[j], pw[j]) for j in range(n)]
    b = SOLVE_BLOCK
    while b < T:
        m_off = same_block(2 * b) & jnp.logical_not(same_block(b))
        t = [_wdot(p[j], jnp.where(m_off, a_list[j], 0.0)) for j in range(n)]
        p = [p[j] - _wdot(t[j], p[j]) for j in range(n)]
        b *= 2
    return p


def _n_squarings(n_valid):
    n_sq = 0
    while (1 << (n_sq + 1)) < n_valid:
        n_sq += 1
    return n_sq


def _gdn_kernel(qraw_ref, kraw_ref, vraw_ref, gate_ref, small_ref, wq_ref, wk_ref, wv_ref, alog_ref, alogt_ref,
                dtb_ref, dtbt_ref, norm_ref, s0_ref, o_ref, s_ref, bufq, bufk, bufv, q_s, k_s, v_s):
    T = T_PAD
    K = GDN_HEAD_DIM
    W = 2 * K

    @pl.when(pl.program_id(1) == 0)
    def _():
        s_ref[...] = s0_ref[...]

    row = lax.broadcasted_iota(jnp.int32, (T, T), 0)
    col = lax.broadcasted_iota(jnp.int32, (T, T), 1)
    tri_l = (row >= col).astype(F32)
    tri_u = (row <= col).astype(F32)
    row_w = lax.broadcasted_iota(jnp.int32, (T, W), 0)
    lane_w = lax.broadcasted_iota(jnp.int32, (T, W), 1)
    col_w = lane_w % K
    first = lane_w < K
    incl_w = row_w >= col_w
    strict_w = row_w > col_w
    eye_w = (row_w == col_w).astype(F32)
    same_block = lambda b: (row_w // b) == (col_w // b)

    first_chunk = pl.program_id(1) == 0
    q_s[...] = _silu(_causal_conv(qraw_ref, bufq, wq_ref, T, first_chunk))
    k_s[...] = _silu(_causal_conv(kraw_ref, bufk, wk_ref, T, first_chunk))
    v_s[...] = _silu(_causal_conv(vraw_ref, bufv, wv_ref, T, first_chunk))
    q_ref, k_ref, v_ref = q_s, k_s, v_s
    raw, raw_t = _small_and_transpose(small_ref)
    beta = jax.nn.sigmoid(raw[:, S_BETA:S_BETA + GDN_HEADS])
    g = -jnp.exp(alog_ref[...]) * jax.nn.softplus(raw[:, S_A:S_A + GDN_HEADS] + dtb_ref[...])
    gt = -jnp.exp(alogt_ref[...]) * jax.nn.softplus(raw_t[S_A:S_A + GDN_HEADS, :] + dtbt_ref[...])
    gcs = _dot_sel_l(tri_l, g)
    gcst = _dot_sel_r(gt, tri_u)
    egcs = jnp.exp(gcs)
    last = gcs[T - 1:T, :]
    kdec = jnp.exp(last - gcs)
    cd = jnp.exp(last)
    norm_w = jnp.concatenate([norm_ref[...], norm_ref[...]], axis=1)

    def widen(x, p):
        return jnp.where(first[:x.shape[0]], x[:, 2 * p:2 * p + 1], x[:, 2 * p + 1:2 * p + 2])

    def l2n(x):
        sq = x * x
        sa = jnp.sum(sq[:, :K], axis=-1, keepdims=True)
        sb = jnp.sum(sq[:, K:], axis=-1, keepdims=True)
        return x * lax.rsqrt(jnp.where(first, sa, sb) + EPS)

    for p0 in range(0, GDN_PAIRS, GDN_PAIR_BATCH):
        pairs = list(range(p0, p0 + GDN_PAIR_BATCH))
        qs, ks, decays, a_mats = [], [], [], []
        for p in pairs:
            lanes = slice(p * W, (p + 1) * W)
            qs.append(l2n(q_ref[:, lanes]) * (K ** -0.5))
            kw = l2n(k_ref[:, lanes])
            ks.append(kw)
            gcst_w = jnp.concatenate([gcst[2 * p:2 * p + 1, :], gcst[2 * p + 1:2 * p + 2, :]], axis=1)
            decay = jnp.exp(jnp.where(incl_w, widen(gcs, p) - gcst_w, -jnp.inf))
            decays.append(decay)
            a_mats.append(jnp.where(strict_w, widen(beta, p) * _wdot_nt(kw, kw) * decay, 0.0))
        minv = _blocked_unit_lower_inverse(a_mats, eye_w, same_block)
        us = [_wdot(minv[j], v_ref[:, p * W:(p + 1) * W] * widen(beta, p)) for j, p in enumerate(pairs)]
        ws = [_wdot(minv[j], ks[j] * (widen(beta, p) * widen(egcs, p))) for j, p in enumerate(pairs)]
        states = [jnp.concatenate([s_ref[0, 2 * p], s_ref[0, 2 * p + 1]], axis=1) for p in pairs]
        ws_qs = [_wdot(jnp.concatenate([ws[j], qs[j] * widen(egcs, p)], axis=0), states[j])
                 for j, p in enumerate(pairs)]
        v_new = [us[j] - ws_qs[j][:T] for j in range(len(pairs))]
        qk = [_wdot_nt(qs[j], ks[j]) * decays[j] for j in range(len(pairs))]
        for j, p in enumerate(pairs):
            o = ws_qs[j][T:] + _wdot(qk[j], v_new[j])
            kd = ks[j] * widen(kdec, p)
            kd_rows = jnp.concatenate([kd[:, :K], kd[:, K:]], axis=0).astype(BF16)
            upd = lax.dot_general(kd_rows, _bd(v_new[j]), (((0,), (0,)), ((), ())), preferred_element_type=F32)
            s_new = widen(cd, p)[:1] * states[j] + upd
            s_ref[0, 2 * p] = s_new[:, :K]
            s_ref[0, 2 * p + 1] = s_new[:, K:]
            sq = o * o
            ms = jnp.where(first, jnp.mean(sq[:, :K], axis=-1, keepdims=True), jnp.mean(sq[:, K:], axis=-1, keepdims=True))
            lanes = slice(p * W, (p + 1) * W)
            o = o * lax.rsqrt(ms + EPS) * norm_w * _silu(gate_ref[:, lanes])
            o_ref[:, lanes] = o.astype(o_ref.dtype)


def _gdn_params(p):
    cw = p['gdn_conv_w']
    return (cw[:, :D_GDN], cw[:, D_GDN:2 * D_GDN], cw[:, 2 * D_GDN:],
            p['gdn_a_log'].reshape(1, GDN_HEADS), p['gdn_a_log'].reshape(GDN_HEADS, 1),
            p['gdn_dt_bias'].reshape(1, GDN_HEADS), p['gdn_dt_bias'].reshape(GDN_HEADS, 1),
            p['gdn_norm'].reshape(1, GDN_HEAD_DIM))


def _gdn_specs(tq, nc, layer):
    return [_proj_block(tq, nc, D_GDN, P_Q), _proj_block(tq, nc, D_GDN, P_K), _proj_block(tq, nc, D_GDN, P_V),
            _proj_block(tq, nc, D_GDN, P_GATE), _proj_block(tq, nc, LANES, P_SMALL),
            _full_block((CONV_K, D_GDN)), _full_block((CONV_K, D_GDN)), _full_block((CONV_K, D_GDN)),
            _full_block((1, GDN_HEADS)), _full_block((GDN_HEADS, 1)), _full_block((1, GDN_HEADS)), _full_block((GDN_HEADS, 1)),
            _full_block((1, GDN_HEAD_DIM)),
            pl.BlockSpec((None, 1, GDN_HEADS, GDN_HEAD_DIM, GDN_HEAD_DIM), lambda b, c: (layer, b, 0, 0, 0))]


def gdn_scan(proj, p, s0, layer, *, nb, nc):
    tq = T_PAD
    rows = nb * nc * tq
    state = pl.BlockSpec((1, GDN_HEADS, GDN_HEAD_DIM, GDN_HEAD_DIM), lambda b, c: (b, 0, 0, 0))
    conv_buf = pltpu.VMEM((tq + CONV_HIST, D_GDN), F32)
    act = pltpu.VMEM((tq, D_GDN), F32)
    return pl.pallas_call(
        _gdn_kernel,
        grid=(nb, nc),
        in_specs=_gdn_specs(tq, nc, layer),
        out_specs=[pl.BlockSpec((tq, D_GDN), lambda b, c: (b * nc + c, 0)), state],
        out_shape=[jax.ShapeDtypeStruct((rows, D_GDN), BF16),
                   jax.ShapeDtypeStruct((nb, GDN_HEADS, GDN_HEAD_DIM, GDN_HEAD_DIM), F32)],
        scratch_shapes=[conv_buf, conv_buf, conv_buf, act, act, act],
        compiler_params=_cparams(("arbitrary", "arbitrary")),
        name="gdn_scan",
    )(proj, proj, proj, proj, proj, *_gdn_params(p), s0)


def _gdn_decode_kernel(qraw_ref, kraw_ref, vraw_ref, gate_ref, small_ref, wq_ref, wk_ref, wv_ref, alog_ref, alogt_ref,
                       dtb_ref, dtbt_ref, norm_ref, s0_ref, o_ref, s_ref, bufq, bufk, bufv, q_s, k_s, v_s, *, valid):
    R = DEC_ROWS
    K = GDN_HEAD_DIM
    N = GDN_HEADS * R

    def stack(ref):
        return jnp.concatenate([ref[:, h * K:(h + 1) * K] for h in range(GDN_HEADS)], axis=0)

    row = lax.broadcasted_iota(jnp.int32, (N, N), 0)
    col = lax.broadcasted_iota(jnp.int32, (N, N), 1)
    same = (row // R) == (col // R)
    incl = same & (row >= col)
    strict = same & (row > col)
    eye = row == col
    eye_f = eye.astype(F32)

    def to_col(rowvec):
        return jnp.sum(jnp.where(eye, rowvec, 0.0), axis=1, keepdims=True)

    first_chunk = pl.program_id(1) == 0
    q_s[...] = _silu(_causal_conv(qraw_ref, bufq, wq_ref, R, first_chunk))
    k_s[...] = _silu(_causal_conv(kraw_ref, bufk, wk_ref, R, first_chunk))
    v_s[...] = _silu(_causal_conv(vraw_ref, bufv, wv_ref, R, first_chunk))
    q_ref, k_ref, v_ref = q_s, k_s, v_s
    _, raw_t = _small_and_transpose(small_ref)
    _, cmask = _valid_masks(valid, GDN_HEADS)
    beta_t = jnp.where(cmask, jax.nn.sigmoid(raw_t[S_BETA:S_BETA + GDN_HEADS, :]), 0.0)
    g_t = jnp.where(cmask, -jnp.exp(alogt_ref[...]) * jax.nn.softplus(raw_t[S_A:S_A + GDN_HEADS, :] + dtbt_ref[...]), 0.0)
    t_row = lax.broadcasted_iota(jnp.int32, (T_PAD, N), 0)
    t_col = lax.broadcasted_iota(jnp.int32, (T_PAD, N), 1)
    tile = (t_row == t_col % R).astype(F32)
    own = lax.broadcasted_iota(jnp.int32, (GDN_HEADS, N), 0) == lax.broadcasted_iota(jnp.int32, (GDN_HEADS, N), 1) // R
    flat = lambda m: jnp.sum(jnp.where(own, _dot_sel_r(m, tile), 0.0), axis=0, keepdims=True)
    g_row = flat(g_t)
    gb = jnp.concatenate([g_row, jnp.zeros((7, N), F32)], axis=0)
    csum = _dot_sel_r(gb, (same & (row <= col)).astype(F32))
    tot = _dot_sel_r(gb, same.astype(F32))
    gcs_row = csum[0:1, :]
    last_row = tot[0:1, :]
    gcs_col = to_col(gcs_row)
    bcol = to_col(flat(beta_t))
    egcs_col = jnp.exp(gcs_col)
    kdec_col = jnp.exp(to_col(last_row) - gcs_col)
    cd_row = jnp.exp(last_row)

    q_all = stack(q_ref)
    k_all = stack(k_ref)
    v_all = stack(v_ref)
    q_all = q_all * lax.rsqrt(jnp.sum(q_all * q_all, axis=-1, keepdims=True) + EPS) * (K ** -0.5)
    k_all = k_all * lax.rsqrt(jnp.sum(k_all * k_all, axis=-1, keepdims=True) + EPS)

    decay = jnp.exp(jnp.where(incl, gcs_col - gcs_row, -jnp.inf))
    a_mat = jnp.where(strict, bcol * _dot_nt(k_all, k_all) * decay, 0.0)
    minv = eye_f - a_mat
    n_sq = _n_squarings(valid[1] - valid[0])
    pw = a_mat
    for _ in range(n_sq):
        pw = _dot(pw, pw)
        minv = minv + _dot(minv, pw)
    u_all = _dot(minv, v_all * bcol)
    w_all = _dot(minv, k_all * (bcol * egcs_col))
    qd_all = q_all * egcs_col
    qk = _dot_nt(q_all, k_all) * decay
    kd_t = _dot_nt(eye_f[:K, :K], k_all * kdec_col).astype(BF16)

    first = lax.broadcasted_iota(jnp.int32, (K, 2 * K), 1) < K
    states, v_new, q_s = [], [], []
    for p in range(GDN_PAIRS):
        ra = slice(2 * p * R, (2 * p + 1) * R)
        rb = slice((2 * p + 1) * R, (2 * p + 2) * R)
        s_w = jnp.concatenate([s0_ref[0, 2 * p], s0_ref[0, 2 * p + 1]], axis=1)
        states.append(s_w)
        lhs = jnp.concatenate([jnp.concatenate([w_all[ra], w_all[rb]], axis=1),
                               jnp.concatenate([qd_all[ra], qd_all[rb]], axis=1)], axis=0)
        both = _wdot(lhs, s_w)
        v_new += [u_all[ra] - both[:R, :K], u_all[rb] - both[:R, K:]]
        q_s += [both[R:, :K], both[R:, K:]]
    v_new_all = jnp.concatenate(v_new, axis=0)
    o = jnp.concatenate(q_s, axis=0) + _dot(qk, v_new_all)
    ms = jnp.mean(o * o, axis=-1, keepdims=True)
    o = o * lax.rsqrt(ms + EPS) * norm_ref[...] * _silu(stack(gate_ref))
    for h in range(GDN_HEADS):
        o_ref[:, h * K:(h + 1) * K] = o[h * R:(h + 1) * R].astype(o_ref.dtype)
    zero = jnp.zeros((R, 2 * K), BF16)
    zhalf = jnp.zeros((R, K), BF16)
    for p in range(GDN_PAIRS):
        va = jnp.concatenate([v_new[2 * p].astype(BF16), zhalf], axis=1)
        vb = jnp.concatenate([zhalf, v_new[2 * p + 1].astype(BF16)], axis=1)
        rhs = jnp.concatenate([zero] * (2 * p) + [va, vb] + [zero] * (GDN_HEADS - 2 - 2 * p), axis=0)
        upd = jnp.dot(kd_t, rhs, preferred_element_type=F32)
        cd_w = jnp.where(first, cd_row[:, 2 * p * R:2 * p * R + 1], cd_row[:, (2 * p + 1) * R:(2 * p + 1) * R + 1])
        s_new = cd_w * states[p] + upd
        s_ref[0, 2 * p] = s_new[:, :K]
        s_ref[0, 2 * p + 1] = s_new[:, K:]


def gdn_decode(proj, p, s0, layer, *, nb, valid):
    tq = DEC_ROWS
    rows = nb * tq
    state = pl.BlockSpec((1, GDN_HEADS, GDN_HEAD_DIM, GDN_HEAD_DIM), lambda b, c: (b, 0, 0, 0))
    conv_buf = pltpu.VMEM((tq + CONV_HIST, D_GDN), F32)
    act = pltpu.VMEM((tq, D_GDN), F32)
    return pl.pallas_call(
        functools.partial(_gdn_decode_kernel, valid=valid),
        grid=(nb, 1),
        in_specs=_gdn_specs(tq, 1, layer),
        out_specs=[pl.BlockSpec((tq, D_GDN), lambda b, c: (b, 0)), state],
        out_shape=[jax.ShapeDtypeStruct((rows, D_GDN), BF16),
                   jax.ShapeDtypeStruct((nb, GDN_HEADS, GDN_HEAD_DIM, GDN_HEAD_DIM), F32)],
        scratch_shapes=[conv_buf, conv_buf, conv_buf, act, act, act],
        compiler_params=_cparams(("arbitrary", "arbitrary")),
        name="gdn_decode",
    )(proj, proj, proj, proj, proj, *_gdn_params(p), s0)


DEC_TOKEN_ROW = 8


def _mixers(proj, bsz, seq, st_ssm, st_gdn, layer, p):
    if seq >= CHUNK:
        nc, tq, valid = seq // CHUNK, T_PAD, (0, T_PAD)
    else:
        nc, tq, valid = 1, DEC_ROWS, (DEC_TOKEN_ROW, DEC_TOKEN_ROW + seq)
    y, ssm_new = ssd_scan(proj, p, st_ssm, layer, nb=bsz, nc=nc, tq=tq, valid=valid)
    if seq >= CHUNK:
        o, gdn_new = gdn_scan(proj, p, st_gdn, layer, nb=bsz, nc=nc)
    else:
        o, gdn_new = gdn_decode(proj, p, st_gdn, layer, nb=bsz, valid=valid)
    return y, o, ssm_new.reshape(bsz, SSM_HEADS, SSM_HEAD_DIM, D_STATE), gdn_new


def _conv_cols_to_ref_order(rows):
    ssm = jnp.concatenate([rows[..., P_X:P_X + D_SSM], rows[..., P_B:P_B + SSM_GN], rows[..., P_C:P_C + SSM_GN]], axis=-1)
    return ssm, rows[..., P_Q:P_Q + GDN_CONV_DIM]


def _decode_blocks(proj_s, bsz, seq, st_ssm_conv, st_gdn_conv):
    hist = jnp.zeros((bsz, CONV_K - 1, P_COLS), F32)
    hist = hist.at[..., P_X:P_X + D_SSM].set(st_ssm_conv[..., :D_SSM])
    hist = hist.at[..., P_B:P_B + SSM_GN].set(st_ssm_conv[..., D_SSM:D_SSM + SSM_GN])
    hist = hist.at[..., P_C:P_C + SSM_GN].set(st_ssm_conv[..., D_SSM + SSM_GN:])
    hist = hist.at[..., P_Q:P_Q + GDN_CONV_DIM].set(st_gdn_conv)
    lead = jnp.zeros((bsz, DEC_TOKEN_ROW - (CONV_K - 1), P_COLS), F32)
    trail = jnp.zeros((bsz, DEC_ROWS - DEC_TOKEN_ROW - seq, P_COLS), F32)
    blocks = jnp.concatenate([lead, hist, proj_s.reshape(bsz, seq, P_COLS), trail], axis=1)
    return blocks.reshape(bsz * DEC_ROWS, P_COLS)


def _reorder_w_in(w):
    small = jnp.concatenate([w[:, SPLIT_XBC:SPLIT_DT], w[:, SPLIT_GATE:]], axis=1)
    small = jnp.pad(small, ((0, 0), (0, P_COLS - P_SMALL - small.shape[1])))
    xbc = SPLIT_Z
    return jnp.concatenate([w[:, :SPLIT_Z], w[:, xbc:xbc + D_SSM], w[:, SPLIT_DT:SPLIT_GATE],
                            w[:, xbc + D_SSM:xbc + D_SSM + 2 * SSM_GN], small], axis=1).astype(BF16)


def _dense_ffn(h, x, wg, wu, wd):
    m = h.shape[0]
    tm = m // 4
    sub = tm // 4
    nt = m // tm
    te = jnp.zeros((nt,), jnp.int32)
    tmap = jnp.arange(nt, dtype=jnp.int32)
    nv = jnp.full((nt,), tm // sub, jnp.int32)
    hid = ffn_up(h, wg, wu, te, tmap, nv, tm=tm, sub=sub, gr=sub, tn=256)
    return ffn_down(hid, wd, te, tmap, nv, x, tm=tm, sub=sub, gr=sub, tn=1024, tk=1024)


MOE_TILE = 2304
MOE_SUB = 512
MOE_GRANULE = 128


def _moe_ffn(h, x, route, wg, wu, wd):
    m = h.shape[0]
    tm, sub, gr = MOE_TILE, MOE_SUB, MOE_GRANULE
    nt = (m * TOP_K) // tm + N_EXPERTS
    top_i = route[:, :TOP_K].astype(jnp.int32)
    probs = route[:, TOP_K:2 * TOP_K]
    e_flat = top_i.reshape(-1)
    n_assign = e_flat.shape[0]
    onehot = (e_flat[:, None] == jnp.arange(N_EXPERTS, dtype=jnp.int32)[None, :]).astype(jnp.int32)
    csum = jnp.cumsum(onehot, axis=0)
    counts = csum[-1]
    rank = jnp.sum(onehot * csum, axis=1) - 1
    tiles_per_e = (counts + tm - 1) // tm
    tile_end_e = jnp.cumsum(tiles_per_e)
    tile_start_e = tile_end_e - tiles_per_e
    pos = jnp.sum(onehot * tile_start_e[None, :], axis=1) * tm + rank
    src = jnp.zeros((nt * tm,), jnp.int32).at[pos].set(jnp.arange(n_assign, dtype=jnp.int32) // TOP_K)
    n_used = tile_end_e[-1]
    tile_ids = jnp.arange(nt, dtype=jnp.int32)
    tmap = jnp.minimum(tile_ids, n_used - 1)
    te = jnp.minimum(jnp.sum((tile_end_e[None, :] <= tmap[:, None]).astype(jnp.int32), axis=1), N_EXPERTS - 1)
    rows_in_tile = jnp.clip(counts[te] - (tmap - tile_start_e[te]) * tm, 0, tm)
    nv = jnp.where(tile_ids < n_used, (rows_in_tile + gr - 1) // gr, 0).astype(jnp.int32)

    xs = h.at[src].get(mode="promise_in_bounds")
    hid = ffn_up(xs, wg, wu, te, tmap, nv, tm=tm, sub=sub, gr=gr, tn=256)
    ys = ffn_down(hid, wd, te, tmap, nv, tm=tm, sub=sub, gr=gr, tn=1024, tk=1024)
    pos = pos.reshape(m, TOP_K)
    out = probs[:, 0:1] * ys.at[pos[:, 0]].get(mode="promise_in_bounds")
    out = out + probs[:, 1:2] * ys.at[pos[:, 1]].get(mode="promise_in_bounds")
    return x + out


def kernel(x_prompt, x_sample, state_ssm, state_ssm_conv, state_gdn, state_gdn_conv, norm_mix, w_in, ssm_conv_w, ssm_conv_b, ssm_dt_bias, ssm_a_log, ssm_d, ssm_norm, gdn_conv_w, gdn_dt_bias, gdn_a_log, gdn_norm, w_out, norm_ffn, dense_w_gate, dense_w_up, dense_w_down, moe_router, moe_w_gate, moe_w_up, moe_w_down, norm_final):
    bp, lp, d = x_prompt.shape
    bs, ls, _ = x_sample.shape
    mp, ms = bp * lp, bs * ls
    x = jnp.concatenate([x_prompt.reshape(mp, d), x_sample.reshape(ms, d)], axis=0)
    m = mp + ms
    tm_all = m // 8

    tm_out = 1024
    zero_ssm = jnp.zeros((1, bp, D_SSM, D_STATE), F32)
    zero_gdn = jnp.zeros((1, bp, GDN_HEADS, GDN_HEAD_DIM, GDN_HEAD_DIM), F32)
    ssm_states = state_ssm.reshape(DEPTH, bs, D_SSM, D_STATE)
    new_p = [[], [], [], []]
    new_s = [[], [], [], []]
    for i in range(DEPTH):
        p = dict(ssm_conv_w=ssm_conv_w[i], ssm_conv_b=ssm_conv_b[i], ssm_dt_bias=ssm_dt_bias[i],
                 ssm_a_log=ssm_a_log[i], ssm_d=ssm_d[i], ssm_norm=ssm_norm[i], gdn_conv_w=gdn_conv_w[i],
                 gdn_dt_bias=gdn_dt_bias[i], gdn_a_log=gdn_a_log[i], gdn_norm=gdn_norm[i])
        h = rmsnorm(x, norm_mix[i], BF16)
        proj = matmul(h, _reorder_w_in(w_in[i]), tm=tm_all, tn=512, sub=tm_all // 2)
        y_p, o_p, ssm_p, gdn_p = _mixers(proj, bp, lp, zero_ssm, zero_gdn, 0, p)
        proj_s = proj[mp:]
        y_s, o_s, ssm_s, gdn_s = _mixers(_decode_blocks(proj_s, bs, ls, state_ssm_conv[i], state_gdn_conv[i]),
                                         bs, ls, ssm_states, state_gdn, i, p)
        tail_p = jnp.stack([proj[b * lp + lp - (CONV_K - 1):(b + 1) * lp] for b in range(bp)])
        conv_p = _conv_cols_to_ref_order(tail_p)
        ssm_tail, gdn_tail = _conv_cols_to_ref_order(proj_s.reshape(bs, ls, P_COLS))
        conv_s = (jnp.concatenate([state_ssm_conv[i], ssm_tail], axis=1)[:, ls:],
                  jnp.concatenate([state_gdn_conv[i], gdn_tail], axis=1)[:, ls:])
        for k, (vp, vs) in enumerate(((ssm_p, ssm_s), (conv_p[0], conv_s[0]), (gdn_p, gdn_s), (conv_p[1], conv_s[1]))):
            new_p[k].append(vp)
            new_s[k].append(vs)
        tok = lambda t: jnp.pad(t.reshape(bs, DEC_ROWS, -1)[:, DEC_TOKEN_ROW:DEC_TOKEN_ROW + ls].reshape(ms, -1),
                                ((0, tm_out - ms), (0, 0)))
        x = out_proj(y_p, o_p, tok(y_s), tok(o_s), w_out[i].astype(BF16), x, tm=tm_out, tn=512, sub=256)
        j = i // 2
        if i % 2 == 0:
            h = rmsnorm(x, norm_ffn[i], BF16)
            x = _dense_ffn(h, x, dense_w_gate[j:j + 1], dense_w_up[j:j + 1], dense_w_down[j:j + 1])
        else:
            h, route = rmsnorm_router(x, norm_ffn[i], moe_router[j])
            x = _moe_ffn(h, x, route, moe_w_gate[j], moe_w_up[j], moe_w_down[j])
    y = rmsnorm(x, norm_final, F32)
    return (y[:mp].reshape(bp, lp, d), y[mp:].reshape(bs, ls, d),
            jnp.stack(new_p[0]), jnp.stack(new_p[1]), jnp.stack(new_p[2]), jnp.stack(new_p[3]),
            jnp.stack(new_s[0]), jnp.stack(new_s[1]), jnp.stack(new_s[2]), jnp.stack(new_s[3]))
```

```python
import functools

import jax
import jax.numpy as jnp
from jax import lax
from jax.experimental import pallas as pl
from jax.experimental.pallas import tpu as pltpu

F32 = jnp.float32
BF16 = jnp.bfloat16

D_MODEL = 4096
DEPTH = 2
D_SSM = 2048
D_GDN = 2048
SSM_HEAD_DIM = 64
SSM_HEADS = 32
SSM_GROUPS = 4
SSM_HPG = 8
D_STATE = 128
SSM_GN = SSM_GROUPS * D_STATE
SSM_CONV_DIM = D_SSM + 2 * SSM_GN
GDN_HEAD_DIM = 128
GDN_HEADS = 16
GDN_CONV_DIM = 3 * D_GDN
CONV_K = 4
CHUNK = 128
SPLIT_Z = D_SSM
SPLIT_XBC = SPLIT_Z + SSM_CONV_DIM
SPLIT_DT = SPLIT_XBC + SSM_HEADS
SPLIT_QKV = SPLIT_DT + GDN_CONV_DIM
SPLIT_GATE = SPLIT_QKV + D_GDN
SPLIT_BETA = SPLIT_GATE + GDN_HEADS
IN_COLS = SPLIT_BETA + GDN_HEADS
D_FF = 14336
N_EXPERTS = 8
TOP_K = 2
EPS = 1e-6

P_Z, P_X, P_Q, P_K, P_V, P_GATE = 0, 2048, 4096, 6144, 8192, 10240
P_B, P_C, P_SMALL = 12288, 12800, 13312
P_COLS = 13824
S_DT, S_BETA, S_A = 0, SSM_HEADS, SSM_HEADS + GDN_HEADS
CONV_HIST = 8

LANES = 128
T_PAD = 128
DEC_ROWS = 16
VMEM_LIMIT = 56 * 1024 * 1024


def _cparams(sem):
    return pltpu.CompilerParams(dimension_semantics=sem, vmem_limit_bytes=VMEM_LIMIT)


def _dot(a, b):
    return jnp.dot(a.astype(BF16), b.astype(BF16), preferred_element_type=F32)


def _dot_nt(a, b):
    return lax.dot_general(a.astype(BF16), b.astype(BF16), (((1,), (1,)), ((), ())),
                           preferred_element_type=F32)


def _dot_tn(a, b):
    return lax.dot_general(a.astype(BF16), b.astype(BF16), (((0,), (0,)), ((), ())),
                           preferred_element_type=F32)


def _split3(a):
    a1 = a.astype(BF16)
    r1 = a - a1.astype(F32)
    a2 = r1.astype(BF16)
    a3 = (r1 - a2.astype(F32)).astype(BF16)
    return a1, a2, a3


def _dot_sel_l(sel, a):
    a1, a2, a3 = _split3(a)
    s = sel.astype(BF16)
    d = lambda p: jnp.dot(s, p, preferred_element_type=F32)
    return (d(a3) + d(a2)) + d(a1)


def _dot_sel_r(a, sel):
    a1, a2, a3 = _split3(a)
    s = sel.astype(BF16)
    d = lambda p: jnp.dot(p, s, preferred_element_type=F32)
    return (d(a3) + d(a2)) + d(a1)


def _dot_hi(a, b):
    a1 = a.astype(BF16)
    a2 = (a - a1.astype(F32)).astype(BF16)
    b1 = b.astype(BF16)
    b2 = (b - b1.astype(F32)).astype(BF16)
    d = lambda p, q: jnp.dot(p, q, preferred_element_type=F32)
    return (d(a2, b1) + d(a1, b2)) + d(a1, b1)


def _pad_rows(v, rows):
    if v.shape[0] == rows:
        return v
    return jnp.concatenate([v, jnp.zeros((rows - v.shape[0], v.shape[1]), v.dtype)], axis=0)


def _silu(v):
    return v * jax.nn.sigmoid(v)


def _causal_conv(raw_ref, buf_ref, w_ref, tq, first_chunk):
    @pl.when(first_chunk)
    def _():
        buf_ref[0:CONV_HIST, :] = jnp.zeros((CONV_HIST, buf_ref.shape[1]), F32)

    buf_ref[CONV_HIST:CONV_HIST + tq, :] = raw_ref[...]
    w = w_ref[...]
    lo = CONV_HIST - (CONV_K - 1)
    y = w[0:1] * buf_ref[lo:lo + tq, :]
    for j in range(1, CONV_K):
        y = y + w[j:j + 1] * buf_ref[lo + j:lo + j + tq, :]
    buf_ref[0:CONV_HIST, :] = buf_ref[tq:tq + CONV_HIST, :]
    return y


def _small_and_transpose(small_ref):
    raw = _pad_rows(small_ref[...], T_PAD)
    row = lax.broadcasted_iota(jnp.int32, (T_PAD, T_PAD), 0)
    col = lax.broadcasted_iota(jnp.int32, (T_PAD, T_PAD), 1)
    eye = (row == col).astype(BF16)
    a1, a2, a3 = _split3(raw)
    d = lambda p: lax.dot_general(p, eye, (((0,), (0,)), ((), ())), preferred_element_type=F32)
    return raw, (d(a3) + d(a2)) + d(a1)


def _valid_masks(valid, width):
    lo, hi = valid
    r = lax.broadcasted_iota(jnp.int32, (T_PAD, width), 0)
    c = lax.broadcasted_iota(jnp.int32, (width, T_PAD), 1)
    return (r >= lo) & (r < hi), (c >= lo) & (c < hi)


def _rmsnorm_kernel(x_ref, g_ref, o_ref):
    x = x_ref[...]
    ms = jnp.mean(x * x, axis=-1, keepdims=True)
    o_ref[...] = (x * lax.rsqrt(ms + EPS) * g_ref[...]).astype(o_ref.dtype)


def rmsnorm(x, gain, out_dtype, tm=256):
    m, d = x.shape
    return pl.pallas_call(
        _rmsnorm_kernel,
        grid=(m // tm,),
        in_specs=[pl.BlockSpec((tm, d), lambda i: (i, 0)), pl.BlockSpec((1, d), lambda i: (0, 0))],
        out_specs=pl.BlockSpec((tm, d), lambda i: (i, 0)),
        out_shape=jax.ShapeDtypeStruct((m, d), out_dtype),
        compiler_params=_cparams(("parallel",)),
        name="rmsnorm",
    )(x, gain.reshape(1, d))


def _rmsnorm_router_kernel(x_ref, g_ref, r_ref, o_ref, route_ref):
    x = x_ref[...]
    ms = jnp.mean(x * x, axis=-1, keepdims=True)
    h = x * lax.rsqrt(ms + EPS) * g_ref[...]
    o_ref[...] = h.astype(o_ref.dtype)
    logits = _dot_hi(h, r_ref[...])
    lane = lax.broadcasted_iota(jnp.int32, logits.shape, 1)
    neg = jnp.float32(-jnp.inf)
    lm = jnp.where(lane < N_EXPERTS, logits, neg)
    m1 = jnp.max(lm, axis=-1, keepdims=True)
    i1 = jnp.min(jnp.where(lm == m1, lane, LANES), axis=-1, keepdims=True)
    lm2 = jnp.where(lane == i1, neg, lm)
    m2 = jnp.max(lm2, axis=-1, keepdims=True)
    i2 = jnp.min(jnp.where(lm2 == m2, lane, LANES), axis=-1, keepdims=True)
    e = jnp.exp(m2 - m1)
    p1 = 1.0 / (1.0 + e)
    p2 = e / (1.0 + e)
    route_ref[...] = jnp.where(lane == 0, i1.astype(F32),
                               jnp.where(lane == 1, i2.astype(F32),
                                         jnp.where(lane == 2, p1, jnp.where(lane == 3, p2, 0.0))))


def rmsnorm_router(x, gain, router, tm=256):
    m, d = x.shape
    r_pad = jnp.zeros((d, LANES), F32).at[:, :N_EXPERTS].set(router)
    return pl.pallas_call(
        _rmsnorm_router_kernel,
        grid=(m // tm,),
        in_specs=[pl.BlockSpec((tm, d), lambda i: (i, 0)), pl.BlockSpec((1, d), lambda i: (0, 0)),
                  pl.BlockSpec((d, LANES), lambda i: (0, 0))],
        out_specs=[pl.BlockSpec((tm, d), lambda i: (i, 0)), pl.BlockSpec((tm, LANES), lambda i: (i, 0))],
        out_shape=[jax.ShapeDtypeStruct((m, d), BF16), jax.ShapeDtypeStruct((m, LANES), F32)],
        compiler_params=_cparams(("parallel",)),
        name="rmsnorm_router",
    )(x, gain.reshape(1, d), r_pad)


def _matmul_kernel(*refs, sub, nsub, has_res):
    if has_res:
        x_ref, w_ref, r_ref, o_ref = refs
    else:
        x_ref, w_ref, o_ref = refs
        r_ref = None

    def body(i, carry):
        rows = pl.ds(pl.multiple_of(i * sub, sub), sub)
        acc = jnp.dot(x_ref[rows, :], w_ref[...], preferred_element_type=F32)
        if has_res:
            acc = acc + r_ref[rows, :]
        o_ref[rows, :] = acc.astype(o_ref.dtype)
        return carry

    lax.fori_loop(0, nsub, body, 0)


def matmul(x, w, res=None, *, tm, tn, sub, out_dtype=F32):
    m, k = x.shape
    n = w.shape[1]
    in_specs = [pl.BlockSpec((tm, k), lambda i, j: (i, 0)), pl.BlockSpec((k, tn), lambda i, j: (0, j))]
    args = [x, w]
    if res is not None:
        in_specs.append(pl.BlockSpec((tm, tn), lambda i, j: (i, j)))
        args.append(res)
    return pl.pallas_call(
        functools.partial(_matmul_kernel, sub=sub, nsub=tm // sub, has_res=res is not None),
        grid=(m // tm, n // tn),
        in_specs=in_specs,
        out_specs=pl.BlockSpec((tm, tn), lambda i, j: (i, j)),
        out_shape=jax.ShapeDtypeStruct((m, n), out_dtype),
        compiler_params=_cparams(("parallel", "arbitrary")),
        name="matmul",
    )(*args)


def _out_proj_kernel(yp_ref, op_ref, ys_ref, os_ref, wy_ref, wo_ref, r_ref, o_ref, *, sub, nsub, n_prompt_tiles):
    def run(y_ref, g_ref):
        def body(i, carry):
            rows = pl.ds(pl.multiple_of(i * sub, sub), sub)
            acc = jnp.dot(y_ref[rows, :], wy_ref[...], preferred_element_type=F32)
            acc = acc + jnp.dot(g_ref[rows, :], wo_ref[...], preferred_element_type=F32)
            o_ref[rows, :] = acc + r_ref[rows, :]
            return carry

        lax.fori_loop(0, nsub, body, 0)

    is_prompt = pl.program_id(0) < n_prompt_tiles

    @pl.when(is_prompt)
    def _():
        run(yp_ref, op_ref)

    @pl.when(jnp.logical_not(is_prompt))
    def _():
        run(ys_ref, os_ref)


def out_proj(y_p, o_p, y_s, o_s, w, res, *, tm, tn, sub):
    mp, k = y_p.shape
    m, n = res.shape
    npt = mp // tm
    nt = npt + y_s.shape[0] // tm
    p_blk = pl.BlockSpec((tm, k), lambda i, j: (jnp.minimum(i, npt - 1), 0))
    s_blk = pl.BlockSpec((tm, k), lambda i, j: (jnp.maximum(i - npt, 0), 0))
    return pl.pallas_call(
        functools.partial(_out_proj_kernel, sub=sub, nsub=tm // sub, n_prompt_tiles=npt),
        grid=(nt, n // tn),
        in_specs=[p_blk, p_blk, s_blk, s_blk,
                  pl.BlockSpec((k, tn), lambda i, j: (0, j)), pl.BlockSpec((k, tn), lambda i, j: (1, j)),
                  pl.BlockSpec((tm, tn), lambda i, j: (i, j))],
        out_specs=pl.BlockSpec((tm, tn), lambda i, j: (i, j)),
        out_shape=jax.ShapeDtypeStruct((m, n), F32),
        compiler_params=_cparams(("parallel", "arbitrary")),
        name="out_proj",
    )(y_p, o_p, y_s, o_s, w, w, res)


def _row_blocks(nv, sub, gr, big, small):
    per = sub // gr
    n_big = nv // per

    def big_body(i, carry):
        big(pl.multiple_of(i * sub, sub))
        return carry

    lax.fori_loop(0, n_big, big_body, 0)
    if gr < sub:
        def small_body(i, carry):
            small(pl.multiple_of(i * gr, gr))
            return carry

        lax.fori_loop(n_big * per, nv, small_body, 0)


def _ffn_up_kernel(te_ref, tmap_ref, nv_ref, rg_ref, x_ref, wg_ref, wu_ref, o_ref, *, sub, gr, ngr):
    nv = nv_ref[pl.program_id(0)]

    @pl.when(nv > 0)
    def _():
        def block(size):
            def run(r0):
                rows = pl.ds(r0, size)
                xs = x_ref[rows, :]
                g = jnp.dot(xs, wg_ref[0].astype(BF16), preferred_element_type=F32)
                u = jnp.dot(xs, wu_ref[0].astype(BF16), preferred_element_type=F32)
                o_ref[rows, :] = (_silu(g) * u).astype(o_ref.dtype)
            return run

        _row_blocks(nv, sub, gr, block(sub), block(gr))

        def zero(i, carry):
            rows = pl.ds(pl.multiple_of(i * gr, gr), gr)
            o_ref[rows, :] = jnp.zeros((gr, o_ref.shape[1]), o_ref.dtype)
            return carry

        lax.fori_loop(nv, ngr, zero, 0)


def ffn_up(x, wg, wu, tile_e, tile_map, tile_nv, tile_rg, *, tm, sub, gr, tn):
    d = x.shape[1]
    nt = tile_e.shape[0]
    f = wg.shape[2]
    nj = f // tn

    def wmap(t, j, te, tmap, nv, rg):
        return (te[t], 0, jnp.where(nv[t] > 0, j, nj - 1))

    return pl.pallas_call(
        functools.partial(_ffn_up_kernel, sub=sub, gr=gr, ngr=tm // gr),
        grid_spec=pltpu.PrefetchScalarGridSpec(
            num_scalar_prefetch=4,
            grid=(nt, nj),
            in_specs=[pl.BlockSpec((pl.Element(tm), pl.Element(d)), lambda t, j, te, tmap, nv, rg: (rg[t] * LANES, 0),
                                   pipeline_mode=pl.Buffered(1)),
                      pl.BlockSpec((1, d, tn), wmap),
                      pl.BlockSpec((1, d, tn), wmap)],
            out_specs=pl.BlockSpec((tm, tn), lambda t, j, te, tmap, nv, rg: (tmap[t], jnp.where(nv[t] > 0, j, nj - 1))),
        ),
        out_shape=jax.ShapeDtypeStruct((nt * tm, f), BF16),
        compiler_params=_cparams(("arbitrary", "arbitrary")),
        name="ffn_up",
    )(tile_e, tile_map, tile_nv, tile_rg, x, wg, wu)


def _ffn_down_kernel(te_ref, tmap_ref, nv_ref, x_ref, w_ref, *rest, sub, gr, has_res):
    if has_res:
        r_ref, o_ref = rest
    else:
        (o_ref,) = rest
    k = pl.program_id(2)
    nv = nv_ref[pl.program_id(0)]

    @pl.when(nv > 0)
    def _():
        @pl.when(k == 0)
        def _():
            o_ref[...] = r_ref[...] if has_res else jnp.zeros(o_ref.shape, o_ref.dtype)

        def block(size):
            def run(r0):
                rows = pl.ds(r0, size)
                o_ref[rows, :] += jnp.dot(x_ref[rows, :], w_ref[0].astype(BF16), preferred_element_type=F32)
            return run

        _row_blocks(nv, sub, gr, block(sub), block(gr))


def ffn_down(x, wd, tile_e, tile_map, tile_nv, res=None, *, tm, sub, gr, tn, tk):
    rows, f = x.shape
    nt = rows // tm
    d = wd.shape[2]
    nj, nk = d // tn, f // tk

    def live(t, idx, last, nv):
        return jnp.where(nv[t] > 0, idx, last)

    omap = lambda t, j, k, te, tmap, nv: (tmap[t], live(t, j, nj - 1, nv))
    in_specs = [pl.BlockSpec((tm, tk), lambda t, j, k, te, tmap, nv: (tmap[t], live(t, k, nk - 1, nv))),
                pl.BlockSpec((1, tk, tn), lambda t, j, k, te, tmap, nv:
                             (te[t], live(t, k, nk - 1, nv), live(t, j, nj - 1, nv)))]
    args = [tile_e, tile_map, tile_nv, x, wd]
    if res is not None:
        in_specs.append(pl.BlockSpec((tm, tn), omap))
        args.append(res)
    return pl.pallas_call(
        functools.partial(_ffn_down_kernel, sub=sub, gr=gr, has_res=res is not None),
        grid_spec=pltpu.PrefetchScalarGridSpec(
            num_scalar_prefetch=3,
            grid=(nt, nj, nk),
            in_specs=in_specs,
            out_specs=pl.BlockSpec((tm, tn), omap),
        ),
        out_shape=jax.ShapeDtypeStruct((rows, d), F32),
        compiler_params=_cparams(("arbitrary", "arbitrary", "arbitrary")),
        name="ffn_down",
    )(*args)


def _ssd_kernel(z_ref, xraw_ref, braw_ref, craw_ref, small_ref, wx_ref, wb_ref, wc_ref, bx_ref, bb_ref, bc_ref,
                dtb_ref, dtbt_ref, alog_ref, alogt_ref, dexp_ref, norm_ref, h0_ref, *rest, tq, valid):
    T = T_PAD
    y_ref, h_ref, bufx, bufb, bufc = rest[-5:]

    @pl.when(pl.program_id(1) == 0)
    def _():
        h_ref[...] = h0_ref[...]

    row = lax.broadcasted_iota(jnp.int32, (T, T), 0)
    col = lax.broadcasted_iota(jnp.int32, (T, T), 1)
    incl = (row >= col)
    tri_l = incl.astype(F32)
    tri_u = (row <= col).astype(F32)

    first_chunk = pl.program_id(1) == 0
    x_act = _silu(_causal_conv(xraw_ref, bufx, wx_ref, tq, first_chunk) + bx_ref[...])
    b_act = _silu(_causal_conv(braw_ref, bufb, wb_ref, tq, first_chunk) + bb_ref[...])
    c_act = _silu(_causal_conv(craw_ref, bufc, wc_ref, tq, first_chunk) + bc_ref[...])
    raw, raw_t = _small_and_transpose(small_ref)
    rmask, cmask = _valid_masks(valid, SSM_HEADS)
    dt = jnp.where(rmask, jax.nn.softplus(raw[:, S_DT:S_DT + SSM_HEADS] + dtb_ref[...]), 0.0)
    dtt = jnp.where(cmask, jax.nn.softplus(raw_t[S_DT:S_DT + SSM_HEADS, :] + dtbt_ref[...]), 0.0)
    a = dt * (-jnp.exp(alog_ref[...]))
    at = dtt * (-jnp.exp(alogt_ref[...]))
    acs = _dot_sel_l(tri_l, a)
    acst = _dot_sel_r(at, tri_u)
    last = acs[T - 1:T, :]
    e_last = jnp.exp(last)
    wts = jnp.exp(last - acs) * dt

    hp = lax.broadcasted_iota(jnp.int32, (SSM_HEADS, D_SSM), 1) // SSM_HEAD_DIM
    expand = (hp == lax.broadcasted_iota(jnp.int32, (SSM_HEADS, D_SSM), 0)).astype(F32)
    dt_e = _dot_sel_r(dt, expand)
    wts_e = _dot_sel_r(wts, expand)
    eacs_e = _dot_sel_r(jnp.exp(acs[:tq]), expand)

    x = _pad_rows(x_act, T)
    xdt = (x * dt_e).astype(BF16)
    xw = (x * wts_e).astype(BF16)
    incl_q = incl[:tq]
    half = lax.broadcasted_iota(jnp.int32, (tq, LANES), 1) < SSM_HEAD_DIM
    gw = SSM_HPG * SSM_HEAD_DIM

    y_groups = []
    for g in range(SSM_GROUPS):
        bg = _pad_rows(b_act[:, g * D_STATE:(g + 1) * D_STATE], T).astype(BF16)
        cg = c_act[:, g * D_STATE:(g + 1) * D_STATE].astype(BF16)
        cb = _dot_nt(cg, bg)
        hg = h_ref[0, g * gw:(g + 1) * gw, :]
        y_state = _dot_nt(cg, hg)
        pieces = []
        for j in range(SSM_HPG // 2):
            ys = []
            for hh in (2 * j, 2 * j + 1):
                h = g * SSM_HPG + hh
                seg = acs[:tq, h:h + 1] - acst[h:h + 1, :]
                w = cb * jnp.exp(jnp.where(incl_q, seg, -jnp.inf))
                lo = g * gw + j * LANES
                ys.append(jnp.dot(w.astype(BF16), xdt[:, lo:lo + LANES], preferred_element_type=F32))
            pieces.append(jnp.where(half, ys[0], ys[1]))
        y_intra = jnp.concatenate(pieces, axis=1)
        y_groups.append(y_intra + eacs_e[:, g * gw:(g + 1) * gw] * y_state)
        upd = _dot_tn(xw[:, g * gw:(g + 1) * gw], bg)
        for hh in range(SSM_HPG):
            h = g * SSM_HPG + hh
            r0 = g * gw + hh * SSM_HEAD_DIM
            h_ref[0, r0:r0 + SSM_HEAD_DIM, :] = (e_last[:, h:h + 1] * hg[hh * SSM_HEAD_DIM:(hh + 1) * SSM_HEAD_DIM, :]
                                                 + upd[hh * SSM_HEAD_DIM:(hh + 1) * SSM_HEAD_DIM, :])

    y = jnp.concatenate(y_groups, axis=1)
    y = y + dexp_ref[...] * x[:tq]
    y = y * _silu(z_ref[...])
    outs = []
    for g in range(SSM_GROUPS):
        yg = y[:, g * gw:(g + 1) * gw]
        ms = jnp.mean(yg * yg, axis=-1, keepdims=True)
        outs.append(yg * lax.rsqrt(ms + EPS))
    y_ref[...] = (jnp.concatenate(outs, axis=1) * norm_ref[...]).astype(y_ref.dtype)


def _proj_block(tq, nc, width, offset):
    return pl.BlockSpec((tq, width), lambda b, c: (b * nc + c, offset // width))


def _full_block(shape):
    return pl.BlockSpec(shape, lambda b, c: tuple(0 for _ in shape))


def _stacked_state(shape, nb, layer, stack):
    zeros = (0,) * len(shape)
    if stack is None:
        return pl.BlockSpec((1,) + shape, lambda b, c: (b,) + zeros), (nb,) + shape, [], [], {}
    depth, prev = stack
    spec = pl.BlockSpec((None, 1) + shape, lambda b, c: (layer, b) + zeros)
    if prev is None:
        return spec, (depth, nb) + shape, [], [], {}
    return spec, (depth, nb) + shape, [pl.BlockSpec(memory_space=pl.ANY)], [prev], None


def ssd_scan(proj, p, h0, layer, *, nb, nc, tq, valid, stack=None):
    rows = nb * nc * tq
    cw, cb = p['ssm_conv_w'], p['ssm_conv_b'].reshape(1, SSM_CONV_DIM)
    xs, bs, cs = slice(0, D_SSM), slice(D_SSM, D_SSM + SSM_GN), slice(D_SSM + SSM_GN, SSM_CONV_DIM)
    state_in = pl.BlockSpec((None, 1, D_SSM, D_STATE), lambda b, c: (layer, b, 0, 0))
    state, state_shape, extra_specs, extra_args, aliases = _stacked_state((D_SSM, D_STATE), nb, layer, stack)
    in_specs = [_proj_block(tq, nc, D_SSM, P_Z), _proj_block(tq, nc, D_SSM, P_X), _proj_block(tq, nc, SSM_GN, P_B),
                _proj_block(tq, nc, SSM_GN, P_C), _proj_block(tq, nc, LANES, P_SMALL),
                _full_block((CONV_K, D_SSM)), _full_block((CONV_K, SSM_GN)), _full_block((CONV_K, SSM_GN)),
                _full_block((1, D_SSM)), _full_block((1, SSM_GN)), _full_block((1, SSM_GN)),
                _full_block((1, SSM_HEADS)), _full_block((SSM_HEADS, 1)),
                _full_block((1, SSM_HEADS)), _full_block((SSM_HEADS, 1)), _full_block((1, D_SSM)), _full_block((1, D_SSM)),
                state_in] + extra_specs
    if aliases is None:
        aliases = {len(in_specs) - 1: 1}
    return pl.pallas_call(
        functools.partial(_ssd_kernel, tq=tq, valid=valid),
        grid=(nb, nc),
        in_specs=in_specs,
        out_specs=[pl.BlockSpec((tq, D_SSM), lambda b, c: (b * nc + c, 0)), state],
        out_shape=[jax.ShapeDtypeStruct((rows, D_SSM), BF16), jax.ShapeDtypeStruct(state_shape, F32)],
        scratch_shapes=[pltpu.VMEM((tq + CONV_HIST, D_SSM), F32), pltpu.VMEM((tq + CONV_HIST, SSM_GN), F32),
                        pltpu.VMEM((tq + CONV_HIST, SSM_GN), F32)],
        input_output_aliases=aliases,
        compiler_params=_cparams(("arbitrary", "arbitrary")),
        name="ssd_scan",
    )(proj, proj, proj, proj, proj, cw[:, xs], cw[:, bs], cw[:, cs], cb[:, xs], cb[:, bs], cb[:, cs],
      p['ssm_dt_bias'].reshape(1, SSM_HEADS), p['ssm_dt_bias'].reshape(SSM_HEADS, 1),
      p['ssm_a_log'].reshape(1, SSM_HEADS), p['ssm_a_log'].reshape(SSM_HEADS, 1),
      jnp.repeat(p['ssm_d'], SSM_HEAD_DIM).reshape(1, D_SSM), p['ssm_norm'].reshape(1, D_SSM), h0, *extra_args)


GDN_PAIRS = GDN_HEADS // 2
GDN_PAIR_BATCH = 4
SOLVE_BLOCK = 8


def _bd(w):
    w = w.astype(BF16)
    half = w.shape[1] // 2
    z = jnp.zeros((w.shape[0], half), BF16)
    return jnp.concatenate([jnp.concatenate([w[:, :half], z], axis=1),
                            jnp.concatenate([z, w[:, half:]], axis=1)], axis=0)


def _wdot(x, y_wide):
    return jnp.dot(x.astype(BF16), _bd(y_wide), preferred_element_type=F32)


def _wdot_nt(x, y_wide):
    return lax.dot_general(x.astype(BF16), _bd(y_wide), (((1,), (1,)), ((), ())), preferred_element_type=F32)


def _blocked_unit_lower_inverse(a_list, eye_w, same_block):
    T = eye_w.shape[0]
    n = len(a_list)
    m0 = same_block(SOLVE_BLOCK)
    a_d = [jnp.where(m0, a, 0.0) for a in a_list]
    p = [eye_w - a for a in a_d]
    n_sq = _n_squarings(SOLVE_BLOCK)
    if n_sq > 0:
        pw = [_wdot(a, a) for a in a_d]
        for i in range(n_sq):
            if i + 1 < n_sq:
                r = [_wdot(jnp.concatenate([p[j], pw[j]], axis=0), pw[j]) for j in range(n)]
                p = [p[j] + r[j][:T] for j in range(n)]
                pw = [r[j][T:] for j in range(n)]
            else:
                p = [p[j] + _wdot(p[j], pw[j]) for j in range(n)]
    b = SOLVE_BLOCK
    while b < T:
        m_off = same_block(2 * b) & jnp.logical_not(same_block(b))
        t = [_wdot(p[j], jnp.where(m_off, a_list[j], 0.0)) for j in range(n)]
        p = [p[j] - _wdot(t[j], p[j]) for j in range(n)]
        b *= 2
    return p


def _n_squarings(n_valid):
    n_sq = 0
    while (1 << (n_sq + 1)) < n_valid:
        n_sq += 1
    return n_sq


def _gdn_kernel(qraw_ref, kraw_ref, vraw_ref, gate_ref, small_ref, wq_ref, wk_ref, wv_ref, alog_ref, alogt_ref,
                dtb_ref, dtbt_ref, norm_ref, s0_ref, o_ref, s_ref, bufq, bufk, bufv, q_s, k_s, v_s):
    T = T_PAD
    K = GDN_HEAD_DIM
    W = 2 * K

    @pl.when(pl.program_id(1) == 0)
    def _():
        s_ref[...] = s0_ref[...]

    row = lax.broadcasted_iota(jnp.int32, (T, T), 0)
    col = lax.broadcasted_iota(jnp.int32, (T, T), 1)
    tri_l = (row >= col).astype(F32)
    tri_u = (row <= col).astype(F32)
    row_w = lax.broadcasted_iota(jnp.int32, (T, W), 0)
    lane_w = lax.broadcasted_iota(jnp.int32, (T, W), 1)
    col_w = lane_w % K
    first = lane_w < K
    incl_w = row_w >= col_w
    strict_w = row_w > col_w
    eye_w = (row_w == col_w).astype(F32)
    same_block = lambda b: (row_w // b) == (col_w // b)

    first_chunk = pl.program_id(1) == 0
    q_s[...] = _silu(_causal_conv(qraw_ref, bufq, wq_ref, T, first_chunk))
    k_s[...] = _silu(_causal_conv(kraw_ref, bufk, wk_ref, T, first_chunk))
    v_s[...] = _silu(_causal_conv(vraw_ref, bufv, wv_ref, T, first_chunk))
    q_ref, k_ref, v_ref = q_s, k_s, v_s
    raw, raw_t = _small_and_transpose(small_ref)
    beta = jax.nn.sigmoid(raw[:, S_BETA:S_BETA + GDN_HEADS])
    g = -jnp.exp(alog_ref[...]) * jax.nn.softplus(raw[:, S_A:S_A + GDN_HEADS] + dtb_ref[...])
    gt = -jnp.exp(alogt_ref[...]) * jax.nn.softplus(raw_t[S_A:S_A + GDN_HEADS, :] + dtbt_ref[...])
    gcs = _dot_sel_l(tri_l, g)
    gcst = _dot_sel_r(gt, tri_u)
    egcs = jnp.exp(gcs)
    last = gcs[T - 1:T, :]
    kdec = jnp.exp(last - gcs)
    cd = jnp.exp(last)
    norm_w = jnp.concatenate([norm_ref[...], norm_ref[...]], axis=1)

    def widen(x, p):
        return jnp.where(first[:x.shape[0]], x[:, 2 * p:2 * p + 1], x[:, 2 * p + 1:2 * p + 2])

    def l2n(x):
        sq = x * x
        sa = jnp.sum(sq[:, :K], axis=-1, keepdims=True)
        sb = jnp.sum(sq[:, K:], axis=-1, keepdims=True)
        return x * lax.rsqrt(jnp.where(first, sa, sb) + EPS)

    for p0 in range(0, GDN_PAIRS, GDN_PAIR_BATCH):
        pairs = list(range(p0, p0 + GDN_PAIR_BATCH))
        qs, ks, decays, a_mats = [], [], [], []
        for p in pairs:
            lanes = slice(p * W, (p + 1) * W)
            qs.append(l2n(q_ref[:, lanes]) * (K ** -0.5))
            kw = l2n(k_ref[:, lanes])
            ks.append(kw)
            gcst_w = jnp.concatenate([gcst[2 * p:2 * p + 1, :], gcst[2 * p + 1:2 * p + 2, :]], axis=1)
            decay = jnp.exp(jnp.where(incl_w, widen(gcs, p) - gcst_w, -jnp.inf))
            decays.append(decay)
            a_mats.append(jnp.where(strict_w, widen(beta, p) * _wdot_nt(kw, kw) * decay, 0.0))
        minv = _blocked_unit_lower_inverse(a_mats, eye_w, same_block)
        us = [_wdot(minv[j], v_ref[:, p * W:(p + 1) * W] * widen(beta, p)) for j, p in enumerate(pairs)]
        ws = [_wdot(minv[j], ks[j] * (widen(beta, p) * widen(egcs, p))) for j, p in enumerate(pairs)]
        states = [jnp.concatenate([s_ref[0, 2 * p], s_ref[0, 2 * p + 1]], axis=1) for p in pairs]
        ws_qs = [_wdot(jnp.concatenate([ws[j], qs[j] * widen(egcs, p)], axis=0), states[j])
                 for j, p in enumerate(pairs)]
        v_new = [us[j] - ws_qs[j][:T] for j in range(len(pairs))]
        qk = [_wdot_nt(qs[j], ks[j]) * decays[j] for j in range(len(pairs))]
        for j, p in enumerate(pairs):
            o = ws_qs[j][T:] + _wdot(qk[j], v_new[j])
            kd = ks[j] * widen(kdec, p)
            kd_rows = jnp.concatenate([kd[:, :K], kd[:, K:]], axis=0).astype(BF16)
            upd = lax.dot_general(kd_rows, _bd(v_new[j]), (((0,), (0,)), ((), ())), preferred_element_type=F32)
            s_new = widen(cd, p)[:1] * states[j] + upd
            s_ref[0, 2 * p] = s_new[:, :K]
            s_ref[0, 2 * p + 1] = s_new[:, K:]
            sq = o * o
            ms = jnp.where(first, jnp.mean(sq[:, :K], axis=-1, keepdims=True), jnp.mean(sq[:, K:], axis=-1, keepdims=True))
            lanes = slice(p * W, (p + 1) * W)
            o = o * lax.rsqrt(ms + EPS) * norm_w * _silu(gate_ref[:, lanes])
            o_ref[:, lanes] = o.astype(o_ref.dtype)


def _gdn_params(p):
    cw = p['gdn_conv_w']
    return (cw[:, :D_GDN], cw[:, D_GDN:2 * D_GDN], cw[:, 2 * D_GDN:],
            p['gdn_a_log'].reshape(1, GDN_HEADS), p['gdn_a_log'].reshape(GDN_HEADS, 1),
            p['gdn_dt_bias'].reshape(1, GDN_HEADS), p['gdn_dt_bias'].reshape(GDN_HEADS, 1),
            p['gdn_norm'].reshape(1, GDN_HEAD_DIM))


def _gdn_specs(tq, nc, layer):
    return [_proj_block(tq, nc, D_GDN, P_Q), _proj_block(tq, nc, D_GDN, P_K), _proj_block(tq, nc, D_GDN, P_V),
            _proj_block(tq, nc, D_GDN, P_GATE), _proj_block(tq, nc, LANES, P_SMALL),
            _full_block((CONV_K, D_GDN)), _full_block((CONV_K, D_GDN)), _full_block((CONV_K, D_GDN)),
            _full_block((1, GDN_HEADS)), _full_block((GDN_HEADS, 1)), _full_block((1, GDN_HEADS)), _full_block((GDN_HEADS, 1)),
            _full_block((1, GDN_HEAD_DIM)),
            pl.BlockSpec((None, 1, GDN_HEADS, GDN_HEAD_DIM, GDN_HEAD_DIM), lambda b, c: (layer, b, 0, 0, 0))]


def gdn_scan(proj, p, s0, layer, *, nb, nc):
    tq = T_PAD
    rows = nb * nc * tq
    state = pl.BlockSpec((1, GDN_HEADS, GDN_HEAD_DIM, GDN_HEAD_DIM), lambda b, c: (b, 0, 0, 0))
    conv_buf = pltpu.VMEM((tq + CONV_HIST, D_GDN), F32)
    act = pltpu.VMEM((tq, D_GDN), F32)
    return pl.pallas_call(
        _gdn_kernel,
        grid=(nb, nc),
        in_specs=_gdn_specs(tq, nc, layer),
        out_specs=[pl.BlockSpec((tq, D_GDN), lambda b, c: (b * nc + c, 0)), state],
        out_shape=[jax.ShapeDtypeStruct((rows, D_GDN), BF16),
                   jax.ShapeDtypeStruct((nb, GDN_HEADS, GDN_HEAD_DIM, GDN_HEAD_DIM), F32)],
        scratch_shapes=[conv_buf, conv_buf, conv_buf, act, act, act],
        compiler_params=_cparams(("arbitrary", "arbitrary")),
        name="gdn_scan",
    )(proj, proj, proj, proj, proj, *_gdn_params(p), s0)


def _gdn_decode_kernel(qraw_ref, kraw_ref, vraw_ref, gate_ref, small_ref, wq_ref, wk_ref, wv_ref, alog_ref, alogt_ref,
                       dtb_ref, dtbt_ref, norm_ref, s0_ref, *rest, valid):
    R = DEC_ROWS
    K = GDN_HEAD_DIM
    N = GDN_HEADS * R
    o_ref, s_ref, bufq, bufk, bufv, q_s, k_s, v_s = rest[-8:]

    def stack(ref):
        return jnp.concatenate([ref[:, h * K:(h + 1) * K] for h in range(GDN_HEADS)], axis=0)

    row = lax.broadcasted_iota(jnp.int32, (N, N), 0)
    col = lax.broadcasted_iota(jnp.int32, (N, N), 1)
    same = (row // R) == (col // R)
    incl = same & (row >= col)
    strict = same & (row > col)
    eye = row == col
    eye_f = eye.astype(F32)

    def to_col(rowvec):
        return jnp.sum(jnp.where(eye, rowvec, 0.0), axis=1, keepdims=True)

    first_chunk = pl.program_id(1) == 0
    q_s[...] = _silu(_causal_conv(qraw_ref, bufq, wq_ref, R, first_chunk))
    k_s[...] = _silu(_causal_conv(kraw_ref, bufk, wk_ref, R, first_chunk))
    v_s[...] = _silu(_causal_conv(vraw_ref, bufv, wv_ref, R, first_chunk))
    q_ref, k_ref, v_ref = q_s, k_s, v_s
    _, raw_t = _small_and_transpose(small_ref)
    _, cmask = _valid_masks(valid, GDN_HEADS)
    beta_t = jnp.where(cmask, jax.nn.sigmoid(raw_t[S_BETA:S_BETA + GDN_HEADS, :]), 0.0)
    g_t = jnp.where(cmask, -jnp.exp(alogt_ref[...]) * jax.nn.softplus(raw_t[S_A:S_A + GDN_HEADS, :] + dtbt_ref[...]), 0.0)
    t_row = lax.broadcasted_iota(jnp.int32, (T_PAD, N), 0)
    t_col = lax.broadcasted_iota(jnp.int32, (T_PAD, N), 1)
    tile = (t_row == t_col % R).astype(F32)
    own = lax.broadcasted_iota(jnp.int32, (GDN_HEADS, N), 0) == lax.broadcasted_iota(jnp.int32, (GDN_HEADS, N), 1) // R
    flat = lambda m: jnp.sum(jnp.where(own, _dot_sel_r(m, tile), 0.0), axis=0, keepdims=True)
    g_row = flat(g_t)
    gb = jnp.concatenate([g_row, jnp.zeros((7, N), F32)], axis=0)
    csum = _dot_sel_r(gb, (same & (row <= col)).astype(F32))
    tot = _dot_sel_r(gb, same.astype(F32))
    gcs_row = csum[0:1, :]
    last_row = tot[0:1, :]
    gcs_col = to_col(gcs_row)
    bcol = to_col(flat(beta_t))
    egcs_col = jnp.exp(gcs_col)
    kdec_col = jnp.exp(to_col(last_row) - gcs_col)
    cd_row = jnp.exp(last_row)

    q_all = stack(q_ref)
    k_all = stack(k_ref)
    v_all = stack(v_ref)
    q_all = q_all * lax.rsqrt(jnp.sum(q_all * q_all, axis=-1, keepdims=True) + EPS) * (K ** -0.5)
    k_all = k_all * lax.rsqrt(jnp.sum(k_all * k_all, axis=-1, keepdims=True) + EPS)

    decay = jnp.exp(jnp.where(incl, gcs_col - gcs_row, -jnp.inf))
    a_mat = jnp.where(strict, bcol * _dot_nt(k_all, k_all) * decay, 0.0)
    minv = eye_f - a_mat
    n_sq = _n_squarings(valid[1] - valid[0])
    pw = a_mat
    for _ in range(n_sq):
        pw = _dot(pw, pw)
        minv = minv + _dot(minv, pw)
    u_all = _dot(minv, v_all * bcol)
    w_all = _dot(minv, k_all * (bcol * egcs_col))
    qd_all = q_all * egcs_col
    qk = _dot_nt(q_all, k_all) * decay
    kd_t = _dot_nt(eye_f[:K, :K], k_all * kdec_col).astype(BF16)

    first = lax.broadcasted_iota(jnp.int32, (K, 2 * K), 1) < K
    states, v_new, q_s = [], [], []
    for p in range(GDN_PAIRS):
        ra = slice(2 * p * R, (2 * p + 1) * R)
        rb = slice((2 * p + 1) * R, (2 * p + 2) * R)
        s_w = jnp.concatenate([s0_ref[0, 2 * p], s0_ref[0, 2 * p + 1]], axis=1)
        states.append(s_w)
        lhs = jnp.concatenate([jnp.concatenate([w_all[ra], w_all[rb]], axis=1),
                               jnp.concatenate([qd_all[ra], qd_all[rb]], axis=1)], axis=0)
        both = _wdot(lhs, s_w)
        v_new += [u_all[ra] - both[:R, :K], u_all[rb] - both[:R, K:]]
        q_s += [both[R:, :K], both[R:, K:]]
    v_new_all = jnp.concatenate(v_new, axis=0)
    o = jnp.concatenate(q_s, axis=0) + _dot(qk, v_new_all)
    ms = jnp.mean(o * o, axis=-1, keepdims=True)
    o = o * lax.rsqrt(ms + EPS) * norm_ref[...] * _silu(stack(gate_ref))
    for h in range(GDN_HEADS):
        o_ref[:, h * K:(h + 1) * K] = o[h * R:(h + 1) * R].astype(o_ref.dtype)
    zero = jnp.zeros((R, 2 * K), BF16)
    zhalf = jnp.zeros((R, K), BF16)
    for p in range(GDN_PAIRS):
        va = jnp.concatenate([v_new[2 * p].astype(BF16), zhalf], axis=1)
        vb = jnp.concatenate([zhalf, v_new[2 * p + 1].astype(BF16)], axis=1)
        rhs = jnp.concatenate([zero] * (2 * p) + [va, vb] + [zero] * (GDN_HEADS - 2 - 2 * p), axis=0)
        upd = jnp.dot(kd_t, rhs, preferred_element_type=F32)
        cd_w = jnp.where(first, cd_row[:, 2 * p * R:2 * p * R + 1], cd_row[:, (2 * p + 1) * R:(2 * p + 1) * R + 1])
        s_new = cd_w * states[p] + upd
        s_ref[0, 2 * p] = s_new[:, :K]
        s_ref[0, 2 * p + 1] = s_new[:, K:]


def gdn_decode(proj, p, s0, layer, *, nb, valid, stack=None):
    tq = DEC_ROWS
    rows = nb * tq
    state, state_shape, extra_specs, extra_args, aliases = _stacked_state(
        (GDN_HEADS, GDN_HEAD_DIM, GDN_HEAD_DIM), nb, layer, stack)
    in_specs = _gdn_specs(tq, 1, layer) + extra_specs
    if aliases is None:
        aliases = {len(in_specs) - 1: 1}
    conv_buf = pltpu.VMEM((tq + CONV_HIST, D_GDN), F32)
    act = pltpu.VMEM((tq, D_GDN), F32)
    return pl.pallas_call(
        functools.partial(_gdn_decode_kernel, valid=valid),
        grid=(nb, 1),
        in_specs=in_specs,
        out_specs=[pl.BlockSpec((tq, D_GDN), lambda b, c: (b, 0)), state],
        out_shape=[jax.ShapeDtypeStruct((rows, D_GDN), BF16), jax.ShapeDtypeStruct(state_shape, F32)],
        scratch_shapes=[conv_buf, conv_buf, conv_buf, act, act, act],
        input_output_aliases=aliases,
        compiler_params=_cparams(("arbitrary", "arbitrary")),
        name="gdn_decode",
    )(proj, proj, proj, proj, proj, *_gdn_params(p), s0, *extra_args)


DEC_TOKEN_ROW = 8


def _mixers(proj, bsz, seq, st_ssm, st_gdn, layer, p, stacks=None):
    if seq >= CHUNK:
        y, ssm_new = ssd_scan(proj, p, st_ssm, layer, nb=bsz, nc=seq // CHUNK, tq=T_PAD, valid=(0, T_PAD))
        o, gdn_new = gdn_scan(proj, p, st_gdn, layer, nb=bsz, nc=seq // CHUNK)
        return y, o, ssm_new.reshape(bsz, SSM_HEADS, SSM_HEAD_DIM, D_STATE), gdn_new
    valid = (DEC_TOKEN_ROW, DEC_TOKEN_ROW + seq)
    y, ssm_new = ssd_scan(proj, p, st_ssm, layer, nb=bsz, nc=1, tq=DEC_ROWS, valid=valid, stack=(DEPTH, stacks[0]))
    o, gdn_new = gdn_decode(proj, p, st_gdn, layer, nb=bsz, valid=valid, stack=(DEPTH, stacks[1]))
    return y, o, ssm_new, gdn_new


def _conv_cols_to_ref_order(rows):
    ssm = jnp.concatenate([rows[..., P_X:P_X + D_SSM], rows[..., P_B:P_B + SSM_GN], rows[..., P_C:P_C + SSM_GN]], axis=-1)
    return ssm, rows[..., P_Q:P_Q + GDN_CONV_DIM]


def _decode_blocks(proj_s, bsz, seq, st_ssm_conv, st_gdn_conv):
    hist = jnp.zeros((bsz, CONV_K - 1, P_COLS), F32)
    hist = hist.at[..., P_X:P_X + D_SSM].set(st_ssm_conv[..., :D_SSM])
    hist = hist.at[..., P_B:P_B + SSM_GN].set(st_ssm_conv[..., D_SSM:D_SSM + SSM_GN])
    hist = hist.at[..., P_C:P_C + SSM_GN].set(st_ssm_conv[..., D_SSM + SSM_GN:])
    hist = hist.at[..., P_Q:P_Q + GDN_CONV_DIM].set(st_gdn_conv)
    lead = jnp.zeros((bsz, DEC_TOKEN_ROW - (CONV_K - 1), P_COLS), F32)
    trail = jnp.zeros((bsz, DEC_ROWS - DEC_TOKEN_ROW - seq, P_COLS), F32)
    blocks = jnp.concatenate([lead, hist, proj_s.reshape(bsz, seq, P_COLS), trail], axis=1)
    return blocks.reshape(bsz * DEC_ROWS, P_COLS)


def _reorder_w_in(w):
    small = jnp.concatenate([w[:, SPLIT_XBC:SPLIT_DT], w[:, SPLIT_GATE:]], axis=1)
    small = jnp.pad(small, ((0, 0), (0, P_COLS - P_SMALL - small.shape[1])))
    xbc = SPLIT_Z
    return jnp.concatenate([w[:, :SPLIT_Z], w[:, xbc:xbc + D_SSM], w[:, SPLIT_DT:SPLIT_GATE],
                            w[:, xbc + D_SSM:xbc + D_SSM + 2 * SSM_GN], small], axis=1).astype(BF16)


def _dense_ffn(h, x, wg, wu, wd):
    m = h.shape[0]
    tm = m // 4
    sub = tm // 4
    nt = m // tm
    te = jnp.zeros((nt,), jnp.int32)
    tmap = jnp.arange(nt, dtype=jnp.int32)
    nv = jnp.full((nt,), tm // sub, jnp.int32)
    hid = ffn_up(h, wg, wu, te, tmap, nv, tmap * (tm // LANES), tm=tm, sub=sub, gr=sub, tn=256)
    return ffn_down(hid, wd, te, tmap, nv, x, tm=tm, sub=sub, gr=sub, tn=1024, tk=1024)


MOE_TILE = 3072
MOE_SUB = 512
MOE_GRANULE = LANES


def _moe_ffn(h, x, route, wg, wu, wd):
    m = h.shape[0]
    tm, sub, gr = MOE_TILE, MOE_SUB, MOE_GRANULE
    nt = (m * TOP_K) // tm + N_EXPERTS
    top_i = route[:, :TOP_K].astype(jnp.int32)
    probs = route[:, TOP_K:2 * TOP_K]
    e_flat = top_i.reshape(-1)
    n_assign = e_flat.shape[0]
    onehot = (e_flat[:, None] == jnp.arange(N_EXPERTS, dtype=jnp.int32)[None, :]).astype(jnp.int32)
    csum = jnp.cumsum(onehot, axis=0)
    counts = csum[-1]
    rank = jnp.sum(onehot * csum, axis=1) - 1
    padded = (counts + gr - 1) // gr * gr
    group_start = jnp.cumsum(padded) - padded
    in_pos = jnp.sum(onehot * group_start[None, :], axis=1) + rank
    n_rows = n_assign + N_EXPERTS * gr + tm
    src = jnp.zeros((n_rows,), jnp.int32).at[in_pos].set(jnp.arange(n_assign, dtype=jnp.int32) // TOP_K)
    tiles_per_e = (counts + tm - 1) // tm
    tile_end_e = jnp.cumsum(tiles_per_e)
    tile_start_e = tile_end_e - tiles_per_e
    out_pos = jnp.sum(onehot * tile_start_e[None, :], axis=1) * tm + rank
    n_used = tile_end_e[-1]
    tile_ids = jnp.arange(nt, dtype=jnp.int32)
    tmap = jnp.minimum(tile_ids, n_used - 1)
    te = jnp.minimum(jnp.sum((tile_end_e[None, :] <= tmap[:, None]).astype(jnp.int32), axis=1), N_EXPERTS - 1)
    k_in_e = tmap - tile_start_e[te]
    rows_in_tile = jnp.clip(counts[te] - k_in_e * tm, 0, tm)
    nv = jnp.where(tile_ids < n_used, (rows_in_tile + gr - 1) // gr, 0).astype(jnp.int32)
    rg = ((group_start[te] + k_in_e * tm) // LANES).astype(jnp.int32)

    xs = h.at[src].get(mode="promise_in_bounds")
    hid = ffn_up(xs, wg, wu, te, tmap, nv, rg, tm=tm, sub=sub, gr=gr, tn=256)
    ys = ffn_down(hid, wd, te, tmap, nv, tm=tm, sub=sub, gr=gr, tn=1024, tk=1024)
    out_pos = out_pos.reshape(m, TOP_K)
    out = probs[:, 0:1] * ys.at[out_pos[:, 0]].get(mode="promise_in_bounds")
    out = out + probs[:, 1:2] * ys.at[out_pos[:, 1]].get(mode="promise_in_bounds")
    return x + out


def kernel(x_prompt, x_sample, state_ssm, state_ssm_conv, state_gdn, state_gdn_conv, norm_mix, w_in, ssm_conv_w, ssm_conv_b, ssm_dt_bias, ssm_a_log, ssm_d, ssm_norm, gdn_conv_w, gdn_dt_bias, gdn_a_log, gdn_norm, w_out, norm_ffn, dense_w_gate, dense_w_up, dense_w_down, moe_router, moe_w_gate, moe_w_up, moe_w_down, norm_final):
    bp, lp, d = x_prompt.shape
    bs, ls, _ = x_sample.shape
    mp, ms = bp * lp, bs * ls
    x = jnp.concatenate([x_prompt.reshape(mp, d), x_sample.reshape(ms, d)], axis=0)
    m = mp + ms
    tm_all = m // 8

    tm_out = 1024
    zero_ssm = jnp.zeros((1, bp, D_SSM, D_STATE), F32)
    zero_gdn = jnp.zeros((1, bp, GDN_HEADS, GDN_HEAD_DIM, GDN_HEAD_DIM), F32)
    ssm_states = state_ssm.reshape(DEPTH, bs, D_SSM, D_STATE)
    dec_stacks = (None, None)
    new_p = [[], [], [], []]
    new_s = [[], [], [], []]
    for i in range(DEPTH):
        p = dict(ssm_conv_w=ssm_conv_w[i], ssm_conv_b=ssm_conv_b[i], ssm_dt_bias=ssm_dt_bias[i],
                 ssm_a_log=ssm_a_log[i], ssm_d=ssm_d[i], ssm_norm=ssm_norm[i], gdn_conv_w=gdn_conv_w[i],
                 gdn_dt_bias=gdn_dt_bias[i], gdn_a_log=gdn_a_log[i], gdn_norm=gdn_norm[i])
        h = rmsnorm(x, norm_mix[i], BF16)
        proj = matmul(h, _reorder_w_in(w_in[i]), tm=tm_all, tn=512, sub=tm_all // 2)
        y_p, o_p, ssm_p, gdn_p = _mixers(proj, bp, lp, zero_ssm, zero_gdn, 0, p)
        proj_s = proj[mp:]
        y_s, o_s, ssm_s, gdn_s = _mixers(_decode_blocks(proj_s, bs, ls, state_ssm_conv[i], state_gdn_conv[i]),
                                         bs, ls, ssm_states, state_gdn, i, p, stacks=dec_stacks)
        dec_stacks = (ssm_s, gdn_s)
        tail_p = jnp.stack([proj[b * lp + lp - (CONV_K - 1):(b + 1) * lp] for b in range(bp)])
        conv_p = _conv_cols_to_ref_order(tail_p)
        ssm_tail, gdn_tail = _conv_cols_to_ref_order(proj_s.reshape(bs, ls, P_COLS))
        conv_s = (jnp.concatenate([state_ssm_conv[i], ssm_tail], axis=1)[:, ls:],
                  jnp.concatenate([state_gdn_conv[i], gdn_tail], axis=1)[:, ls:])
        for k, (vp, vs) in enumerate(((ssm_p, None), (conv_p[0], conv_s[0]), (gdn_p, None), (conv_p[1], conv_s[1]))):
            new_p[k].append(vp)
            new_s[k].append(vs)
        tok = lambda t: jnp.pad(t.reshape(bs, DEC_ROWS, -1)[:, DEC_TOKEN_ROW:DEC_TOKEN_ROW + ls].reshape(ms, -1),
                                ((0, tm_out - ms), (0, 0)))
        x = out_proj(y_p, o_p, tok(y_s), tok(o_s), w_out[i].astype(BF16), x, tm=tm_out, tn=512, sub=256)
        j = i // 2
        if i % 2 == 0:
            h = rmsnorm(x, norm_ffn[i], BF16)
            x = _dense_ffn(h, x, dense_w_gate[j:j + 1], dense_w_up[j:j + 1], dense_w_down[j:j + 1])
        else:
            h, route = rmsnorm_router(x, norm_ffn[i], moe_router[j])
            x = _moe_ffn(h, x, route, moe_w_gate[j], moe_w_up[j], moe_w_down[j])
    y = rmsnorm(x, norm_final, F32)
    return (y[:mp].reshape(bp, lp, d), y[mp:].reshape(bs, ls, d),
            jnp.stack(new_p[0]), jnp.stack(new_p[1]), jnp.stack(new_p[2]), jnp.stack(new_p[3]),
            dec_stacks[0].reshape(DEPTH, bs, SSM_HEADS, SSM_HEAD_DIM, D_STATE), jnp.stack(new_s[1]), dec_stacks[1], jnp.stack(new_s[3]))
```

```python
import functools

import jax
import jax.numpy as jnp
from jax import lax
from jax.experimental import pallas as pl
from jax.experimental.pallas import tpu as pltpu

F32 = jnp.float32
BF16 = jnp.bfloat16

D_MODEL = 4096
DEPTH = 2
D_SSM = 2048
D_GDN = 2048
SSM_HEAD_DIM = 64
SSM_HEADS = 32
SSM_GROUPS = 4
SSM_HPG = 8
D_STATE = 128
SSM_GN = SSM_GROUPS * D_STATE
SSM_CONV_DIM = D_SSM + 2 * SSM_GN
GDN_HEAD_DIM = 128
GDN_HEADS = 16
GDN_CONV_DIM = 3 * D_GDN
CONV_K = 4
CHUNK = 128
SPLIT_Z = D_SSM
SPLIT_XBC = SPLIT_Z + SSM_CONV_DIM
SPLIT_DT = SPLIT_XBC + SSM_HEADS
SPLIT_QKV = SPLIT_DT + GDN_CONV_DIM
SPLIT_GATE = SPLIT_QKV + D_GDN
SPLIT_BETA = SPLIT_GATE + GDN_HEADS
IN_COLS = SPLIT_BETA + GDN_HEADS
D_FF = 14336
N_EXPERTS = 8
TOP_K = 2
EPS = 1e-6

P_Z, P_X, P_Q, P_K, P_V, P_GATE = 0, 2048, 4096, 6144, 8192, 10240
P_B, P_C, P_SMALL = 12288, 12800, 13312
P_COLS = 13824
S_DT, S_BETA, S_A = 0, SSM_HEADS, SSM_HEADS + GDN_HEADS
CONV_HIST = 8

LANES = 128
T_PAD = 128
DEC_ROWS = 16
VMEM_LIMIT = 56 * 1024 * 1024


def _cparams(sem):
    return pltpu.CompilerParams(dimension_semantics=sem, vmem_limit_bytes=VMEM_LIMIT)


def _dot(a, b):
    return jnp.dot(a.astype(BF16), b.astype(BF16), preferred_element_type=F32)


def _dot_nt(a, b):
    return lax.dot_general(a.astype(BF16), b.astype(BF16), (((1,), (1,)), ((), ())),
                           preferred_element_type=F32)


def _dot_tn(a, b):
    return lax.dot_general(a.astype(BF16), b.astype(BF16), (((0,), (0,)), ((), ())),
                           preferred_element_type=F32)


def _split3(a):
    a1 = a.astype(BF16)
    r1 = a - a1.astype(F32)
    a2 = r1.astype(BF16)
    a3 = (r1 - a2.astype(F32)).astype(BF16)
    return a1, a2, a3


def _dot_sel_l(sel, a):
    a1, a2, a3 = _split3(a)
    s = sel.astype(BF16)
    d = lambda p: jnp.dot(s, p, preferred_element_type=F32)
    return (d(a3) + d(a2)) + d(a1)


def _dot_sel_r(a, sel):
    a1, a2, a3 = _split3(a)
    s = sel.astype(BF16)
    d = lambda p: jnp.dot(p, s, preferred_element_type=F32)
    return (d(a3) + d(a2)) + d(a1)


def _dot_hi(a, b):
    a1 = a.astype(BF16)
    a2 = (a - a1.astype(F32)).astype(BF16)
    b1 = b.astype(BF16)
    b2 = (b - b1.astype(F32)).astype(BF16)
    d = lambda p, q: jnp.dot(p, q, preferred_element_type=F32)
    return (d(a2, b1) + d(a1, b2)) + d(a1, b1)


def _pad_rows(v, rows):
    if v.shape[0] == rows:
        return v
    return jnp.concatenate([v, jnp.zeros((rows - v.shape[0], v.shape[1]), v.dtype)], axis=0)


def _silu(v):
    return v * jax.nn.sigmoid(v)


def _causal_conv(raw_ref, buf_ref, w_ref, tq, first_chunk):
    @pl.when(first_chunk)
    def _():
        buf_ref[0:CONV_HIST, :] = jnp.zeros((CONV_HIST, buf_ref.shape[1]), F32)

    buf_ref[CONV_HIST:CONV_HIST + tq, :] = raw_ref[...]
    w = w_ref[...]
    lo = CONV_HIST - (CONV_K - 1)
    y = w[0:1] * buf_ref[lo:lo + tq, :]
    for j in range(1, CONV_K):
        y = y + w[j:j + 1] * buf_ref[lo + j:lo + j + tq, :]
    buf_ref[0:CONV_HIST, :] = buf_ref[tq:tq + CONV_HIST, :]
    return y


def _small_and_transpose(small_ref):
    raw = _pad_rows(small_ref[...], T_PAD)
    row = lax.broadcasted_iota(jnp.int32, (T_PAD, T_PAD), 0)
    col = lax.broadcasted_iota(jnp.int32, (T_PAD, T_PAD), 1)
    eye = (row == col).astype(BF16)
    a1, a2, a3 = _split3(raw)
    d = lambda p: lax.dot_general(p, eye, (((0,), (0,)), ((), ())), preferred_element_type=F32)
    return raw, (d(a3) + d(a2)) + d(a1)


def _valid_masks(valid, width):
    lo, hi = valid
    r = lax.broadcasted_iota(jnp.int32, (T_PAD, width), 0)
    c = lax.broadcasted_iota(jnp.int32, (width, T_PAD), 1)
    return (r >= lo) & (r < hi), (c >= lo) & (c < hi)


def _rmsnorm_kernel(x_ref, g_ref, o_ref):
    x = x_ref[...]
    ms = jnp.mean(x * x, axis=-1, keepdims=True)
    o_ref[...] = (x * lax.rsqrt(ms + EPS) * g_ref[...]).astype(o_ref.dtype)


def rmsnorm(x, gain, out_dtype, tm=256):
    m, d = x.shape
    return pl.pallas_call(
        _rmsnorm_kernel,
        grid=(m // tm,),
        in_specs=[pl.BlockSpec((tm, d), lambda i: (i, 0)), pl.BlockSpec((1, d), lambda i: (0, 0))],
        out_specs=pl.BlockSpec((tm, d), lambda i: (i, 0)),
        out_shape=jax.ShapeDtypeStruct((m, d), out_dtype),
        compiler_params=_cparams(("parallel",)),
        name="rmsnorm",
    )(x, gain.reshape(1, d))


def _rmsnorm_router_kernel(x_ref, g_ref, r_ref, o_ref, route_ref):
    x = x_ref[...]
    ms = jnp.mean(x * x, axis=-1, keepdims=True)
    h = x * lax.rsqrt(ms + EPS) * g_ref[...]
    o_ref[...] = h.astype(o_ref.dtype)
    logits = _dot_hi(h, r_ref[...])
    lane = lax.broadcasted_iota(jnp.int32, logits.shape, 1)
    neg = jnp.float32(-jnp.inf)
    lm = jnp.where(lane < N_EXPERTS, logits, neg)
    m1 = jnp.max(lm, axis=-1, keepdims=True)
    i1 = jnp.min(jnp.where(lm == m1, lane, LANES), axis=-1, keepdims=True)
    lm2 = jnp.where(lane == i1, neg, lm)
    m2 = jnp.max(lm2, axis=-1, keepdims=True)
    i2 = jnp.min(jnp.where(lm2 == m2, lane, LANES), axis=-1, keepdims=True)
    e = jnp.exp(m2 - m1)
    p1 = 1.0 / (1.0 + e)
    p2 = e / (1.0 + e)
    route_ref[...] = jnp.where(lane == 0, i1.astype(F32),
                               jnp.where(lane == 1, i2.astype(F32),
                                         jnp.where(lane == 2, p1, jnp.where(lane == 3, p2, 0.0))))


def rmsnorm_router(x, gain, router, tm=256):
    m, d = x.shape
    r_pad = jnp.zeros((d, LANES), F32).at[:, :N_EXPERTS].set(router)
    return pl.pallas_call(
        _rmsnorm_router_kernel,
        grid=(m // tm,),
        in_specs=[pl.BlockSpec((tm, d), lambda i: (i, 0)), pl.BlockSpec((1, d), lambda i: (0, 0)),
                  pl.BlockSpec((d, LANES), lambda i: (0, 0))],
        out_specs=[pl.BlockSpec((tm, d), lambda i: (i, 0)), pl.BlockSpec((tm, LANES), lambda i: (i, 0))],
        out_shape=[jax.ShapeDtypeStruct((m, d), BF16), jax.ShapeDtypeStruct((m, LANES), F32)],
        compiler_params=_cparams(("parallel",)),
        name="rmsnorm_router",
    )(x, gain.reshape(1, d), r_pad)


def _matmul_kernel(*refs, sub, nsub, has_res):
    if has_res:
        x_ref, w_ref, r_ref, o_ref = refs
    else:
        x_ref, w_ref, o_ref = refs
        r_ref = None

    def body(i, carry):
        rows = pl.ds(pl.multiple_of(i * sub, sub), sub)
        acc = jnp.dot(x_ref[rows, :], w_ref[...], preferred_element_type=F32)
        if has_res:
            acc = acc + r_ref[rows, :]
        o_ref[rows, :] = acc.astype(o_ref.dtype)
        return carry

    lax.fori_loop(0, nsub, body, 0)


def matmul(x, w, res=None, *, tm, tn, sub, out_dtype=F32):
    m, k = x.shape
    n = w.shape[1]
    in_specs = [pl.BlockSpec((tm, k), lambda i, j: (i, 0)), pl.BlockSpec((k, tn), lambda i, j: (0, j))]
    args = [x, w]
    if res is not None:
        in_specs.append(pl.BlockSpec((tm, tn), lambda i, j: (i, j)))
        args.append(res)
    return pl.pallas_call(
        functools.partial(_matmul_kernel, sub=sub, nsub=tm // sub, has_res=res is not None),
        grid=(m // tm, n // tn),
        in_specs=in_specs,
        out_specs=pl.BlockSpec((tm, tn), lambda i, j: (i, j)),
        out_shape=jax.ShapeDtypeStruct((m, n), out_dtype),
        compiler_params=_cparams(("parallel", "arbitrary")),
        name="matmul",
    )(*args)


def _out_proj_kernel(yp_ref, op_ref, ys_ref, os_ref, wy_ref, wo_ref, r_ref, o_ref, *, sub, nsub, n_prompt_tiles):
    def run(y_ref, g_ref):
        def body(i, carry):
            rows = pl.ds(pl.multiple_of(i * sub, sub), sub)
            acc = jnp.dot(y_ref[rows, :], wy_ref[...].astype(BF16), preferred_element_type=F32)
            acc = acc + jnp.dot(g_ref[rows, :], wo_ref[...].astype(BF16), preferred_element_type=F32)
            o_ref[rows, :] = acc + r_ref[rows, :]
            return carry

        lax.fori_loop(0, nsub, body, 0)

    is_prompt = pl.program_id(0) < n_prompt_tiles

    @pl.when(is_prompt)
    def _():
        run(yp_ref, op_ref)

    @pl.when(jnp.logical_not(is_prompt))
    def _():
        run(ys_ref, os_ref)


def out_proj(y_p, o_p, y_s, o_s, w, res, *, tm, tn, sub):
    mp, k = y_p.shape
    m, n = res.shape
    npt = mp // tm
    nt = npt + y_s.shape[0] // tm
    p_blk = pl.BlockSpec((tm, k), lambda i, j: (jnp.minimum(i, npt - 1), 0))
    s_blk = pl.BlockSpec((tm, k), lambda i, j: (jnp.maximum(i - npt, 0), 0))
    return pl.pallas_call(
        functools.partial(_out_proj_kernel, sub=sub, nsub=tm // sub, n_prompt_tiles=npt),
        grid=(nt, n // tn),
        in_specs=[p_blk, p_blk, s_blk, s_blk,
                  pl.BlockSpec((k, tn), lambda i, j: (0, j)), pl.BlockSpec((k, tn), lambda i, j: (1, j)),
                  pl.BlockSpec((tm, tn), lambda i, j: (i, j))],
        out_specs=pl.BlockSpec((tm, tn), lambda i, j: (i, j)),
        out_shape=jax.ShapeDtypeStruct((m, n), F32),
        compiler_params=_cparams(("parallel", "arbitrary")),
        name="out_proj",
    )(y_p, o_p, y_s, o_s, w, w, res)


def _row_blocks(nv, sub, gr, big, small):
    per = sub // gr
    n_big = nv // per

    def big_body(i, carry):
        big(pl.multiple_of(i * sub, sub))
        return carry

    lax.fori_loop(0, n_big, big_body, 0)
    if gr < sub:
        def small_body(i, carry):
            small(pl.multiple_of(i * gr, gr))
            return carry

        lax.fori_loop(n_big * per, nv, small_body, 0)


def _ffn_up_kernel(te_ref, tmap_ref, nv_ref, rg_ref, x_ref, wg_ref, wu_ref, o_ref, *, sub, gr, ngr):
    nv = nv_ref[pl.program_id(0)]

    @pl.when(nv > 0)
    def _():
        def block(size):
            def run(r0):
                rows = pl.ds(r0, size)
                xs = x_ref[rows, :]
                g = jnp.dot(xs, wg_ref[0].astype(BF16), preferred_element_type=F32)
                u = jnp.dot(xs, wu_ref[0].astype(BF16), preferred_element_type=F32)
                o_ref[rows, :] = (_silu(g) * u).astype(o_ref.dtype)
            return run

        _row_blocks(nv, sub, gr, block(sub), block(gr))

        def zero(i, carry):
            rows = pl.ds(pl.multiple_of(i * gr, gr), gr)
            o_ref[rows, :] = jnp.zeros((gr, o_ref.shape[1]), o_ref.dtype)
            return carry

        lax.fori_loop(nv, ngr, zero, 0)


def ffn_up(x, wg, wu, tile_e, tile_map, tile_nv, tile_rg, *, tm, sub, gr, tn):
    d = x.shape[1]
    nt = tile_e.shape[0]
    f = wg.shape[2]
    nj = f // tn

    def wmap(t, j, te, tmap, nv, rg):
        return (te[t], 0, jnp.where(nv[t] > 0, j, nj - 1))

    return pl.pallas_call(
        functools.partial(_ffn_up_kernel, sub=sub, gr=gr, ngr=tm // gr),
        grid_spec=pltpu.PrefetchScalarGridSpec(
            num_scalar_prefetch=4,
            grid=(nt, nj),
            in_specs=[pl.BlockSpec((pl.Element(tm), pl.Element(d)), lambda t, j, te, tmap, nv, rg: (rg[t] * LANES, 0),
                                   pipeline_mode=pl.Buffered(1)),
                      pl.BlockSpec((1, d, tn), wmap),
                      pl.BlockSpec((1, d, tn), wmap)],
            out_specs=pl.BlockSpec((tm, tn), lambda t, j, te, tmap, nv, rg: (tmap[t], jnp.where(nv[t] > 0, j, nj - 1))),
        ),
        out_shape=jax.ShapeDtypeStruct((nt * tm, f), BF16),
        compiler_params=_cparams(("arbitrary", "arbitrary")),
        name="ffn_up",
    )(tile_e, tile_map, tile_nv, tile_rg, x, wg, wu)


def _ffn_down_kernel(te_ref, tmap_ref, nv_ref, x_ref, w_ref, *rest, sub, gr, has_res):
    if has_res:
        r_ref, o_ref = rest
    else:
        (o_ref,) = rest
    k = pl.program_id(2)
    nv = nv_ref[pl.program_id(0)]

    @pl.when(nv > 0)
    def _():
        @pl.when(k == 0)
        def _():
            o_ref[...] = r_ref[...] if has_res else jnp.zeros(o_ref.shape, o_ref.dtype)

        def block(size):
            def run(r0):
                rows = pl.ds(r0, size)
                o_ref[rows, :] += jnp.dot(x_ref[rows, :], w_ref[0].astype(BF16), preferred_element_type=F32)
            return run

        _row_blocks(nv, sub, gr, block(sub), block(gr))


def ffn_down(x, wd, tile_e, tile_map, tile_nv, res=None, *, tm, sub, gr, tn, tk):
    rows, f = x.shape
    nt = rows // tm
    d = wd.shape[2]
    nj, nk = d // tn, f // tk

    def live(t, idx, last, nv):
        return jnp.where(nv[t] > 0, idx, last)

    omap = lambda t, j, k, te, tmap, nv: (tmap[t], live(t, j, nj - 1, nv))
    in_specs = [pl.BlockSpec((tm, tk), lambda t, j, k, te, tmap, nv: (tmap[t], live(t, k, nk - 1, nv))),
                pl.BlockSpec((1, tk, tn), lambda t, j, k, te, tmap, nv:
                             (te[t], live(t, k, nk - 1, nv), live(t, j, nj - 1, nv)))]
    args = [tile_e, tile_map, tile_nv, x, wd]
    if res is not None:
        in_specs.append(pl.BlockSpec((tm, tn), omap))
        args.append(res)
    return pl.pallas_call(
        functools.partial(_ffn_down_kernel, sub=sub, gr=gr, has_res=res is not None),
        grid_spec=pltpu.PrefetchScalarGridSpec(
            num_scalar_prefetch=3,
            grid=(nt, nj, nk),
            in_specs=in_specs,
            out_specs=pl.BlockSpec((tm, tn), omap),
        ),
        out_shape=jax.ShapeDtypeStruct((rows, d), F32),
        compiler_params=_cparams(("arbitrary", "arbitrary", "arbitrary")),
        name="ffn_down",
    )(*args)


def _ssd_kernel(z_ref, xraw_ref, braw_ref, craw_ref, small_ref, wx_ref, wb_ref, wc_ref, bx_ref, bb_ref, bc_ref,
                dtb_ref, dtbt_ref, alog_ref, alogt_ref, dexp_ref, norm_ref, h0_ref, *rest, tq, valid):
    T = T_PAD
    y_ref, h_ref, bufx, bufb, bufc = rest[-5:]

    @pl.when(pl.program_id(1) == 0)
    def _():
        h_ref[...] = h0_ref[...]

    row = lax.broadcasted_iota(jnp.int32, (T, T), 0)
    col = lax.broadcasted_iota(jnp.int32, (T, T), 1)
    incl = (row >= col)
    tri_l = incl.astype(F32)
    tri_u = (row <= col).astype(F32)

    first_chunk = pl.program_id(1) == 0
    x_act = _silu(_causal_conv(xraw_ref, bufx, wx_ref, tq, first_chunk) + bx_ref[...])
    b_act = _silu(_causal_conv(braw_ref, bufb, wb_ref, tq, first_chunk) + bb_ref[...])
    c_act = _silu(_causal_conv(craw_ref, bufc, wc_ref, tq, first_chunk) + bc_ref[...])
    raw, raw_t = _small_and_transpose(small_ref)
    rmask, cmask = _valid_masks(valid, SSM_HEADS)
    dt = jnp.where(rmask, jax.nn.softplus(raw[:, S_DT:S_DT + SSM_HEADS] + dtb_ref[...]), 0.0)
    dtt = jnp.where(cmask, jax.nn.softplus(raw_t[S_DT:S_DT + SSM_HEADS, :] + dtbt_ref[...]), 0.0)
    a = dt * (-jnp.exp(alog_ref[...]))
    at = dtt * (-jnp.exp(alogt_ref[...]))
    acs = _dot_sel_l(tri_l, a)
    acst = _dot_sel_r(at, tri_u)
    last = acs[T - 1:T, :]
    e_last = jnp.exp(last)
    wts = jnp.exp(last - acs) * dt

    hp = lax.broadcasted_iota(jnp.int32, (SSM_HEADS, D_SSM), 1) // SSM_HEAD_DIM
    expand = (hp == lax.broadcasted_iota(jnp.int32, (SSM_HEADS, D_SSM), 0)).astype(F32)
    expand_b = expand.astype(BF16)
    lane_expand = lambda v: jnp.dot(v.astype(BF16), expand_b, preferred_element_type=F32)
    eacs_e = lane_expand(jnp.exp(acs[:tq]))
    xdt = _pad_rows((x_act * lane_expand(dt[:tq])).astype(BF16), T)
    xw = _pad_rows((x_act * lane_expand(wts[:tq])).astype(BF16), T)
    incl_q = incl[:tq]
    half = lax.broadcasted_iota(jnp.int32, (tq, LANES), 1) < SSM_HEAD_DIM
    gw = SSM_HPG * SSM_HEAD_DIM

    y_groups = []
    for g in range(SSM_GROUPS):
        bg = _pad_rows(b_act[:, g * D_STATE:(g + 1) * D_STATE], T).astype(BF16)
        cg = c_act[:, g * D_STATE:(g + 1) * D_STATE].astype(BF16)
        cb = _dot_nt(cg, bg)
        hg = h_ref[0, g * gw:(g + 1) * gw, :]
        y_state = _dot_nt(cg, hg)
        pieces = []
        for j in range(SSM_HPG // 2):
            ys = []
            for hh in (2 * j, 2 * j + 1):
                h = g * SSM_HPG + hh
                seg = acs[:tq, h:h + 1] - acst[h:h + 1, :]
                w = cb * jnp.exp(jnp.where(incl_q, seg, -jnp.inf))
                lo = g * gw + j * LANES
                ys.append(jnp.dot(w.astype(BF16), xdt[:, lo:lo + LANES], preferred_element_type=F32))
            pieces.append(jnp.where(half, ys[0], ys[1]))
        y_intra = jnp.concatenate(pieces, axis=1)
        y_groups.append(y_intra + eacs_e[:, g * gw:(g + 1) * gw] * y_state)
        upd = _dot_tn(xw[:, g * gw:(g + 1) * gw], bg)
        for hh in range(SSM_HPG):
            h = g * SSM_HPG + hh
            r0 = g * gw + hh * SSM_HEAD_DIM
            h_ref[0, r0:r0 + SSM_HEAD_DIM, :] = (e_last[:, h:h + 1] * hg[hh * SSM_HEAD_DIM:(hh + 1) * SSM_HEAD_DIM, :]
                                                 + upd[hh * SSM_HEAD_DIM:(hh + 1) * SSM_HEAD_DIM, :])

    y = jnp.concatenate(y_groups, axis=1)
    y = y + dexp_ref[...] * x_act
    y = y * _silu(z_ref[...])
    outs = []
    for g in range(SSM_GROUPS):
        yg = y[:, g * gw:(g + 1) * gw]
        ms = jnp.mean(yg * yg, axis=-1, keepdims=True)
        outs.append(yg * lax.rsqrt(ms + EPS))
    y_ref[...] = (jnp.concatenate(outs, axis=1) * norm_ref[...]).astype(y_ref.dtype)


def _proj_block(tq, nc, width, offset):
    return pl.BlockSpec((tq, width), lambda b, c: (b * nc + c, offset // width))


def _full_block(shape):
    return pl.BlockSpec(shape, lambda b, c: tuple(0 for _ in shape))


def _stacked_state(shape, nb, layer, stack):
    zeros = (0,) * len(shape)
    if stack is None:
        return pl.BlockSpec((1,) + shape, lambda b, c: (b,) + zeros), (nb,) + shape, [], [], {}
    depth, prev = stack
    spec = pl.BlockSpec((None, 1) + shape, lambda b, c: (layer, b) + zeros)
    if prev is None:
        return spec, (depth, nb) + shape, [], [], {}
    return spec, (depth, nb) + shape, [pl.BlockSpec(memory_space=pl.ANY)], [prev], None


def ssd_scan(proj, p, h0, layer, *, nb, nc, tq, valid, stack=None):
    rows = nb * nc * tq
    cw, cb = p['ssm_conv_w'], p['ssm_conv_b'].reshape(1, SSM_CONV_DIM)
    xs, bs, cs = slice(0, D_SSM), slice(D_SSM, D_SSM + SSM_GN), slice(D_SSM + SSM_GN, SSM_CONV_DIM)
    state_in = pl.BlockSpec((None, 1, D_SSM, D_STATE), lambda b, c: (layer, b, 0, 0))
    state, state_shape, extra_specs, extra_args, aliases = _stacked_state((D_SSM, D_STATE), nb, layer, stack)
    in_specs = [_proj_block(tq, nc, D_SSM, P_Z), _proj_block(tq, nc, D_SSM, P_X), _proj_block(tq, nc, SSM_GN, P_B),
                _proj_block(tq, nc, SSM_GN, P_C), _proj_block(tq, nc, LANES, P_SMALL),
                _full_block((CONV_K, D_SSM)), _full_block((CONV_K, SSM_GN)), _full_block((CONV_K, SSM_GN)),
                _full_block((1, D_SSM)), _full_block((1, SSM_GN)), _full_block((1, SSM_GN)),
                _full_block((1, SSM_HEADS)), _full_block((SSM_HEADS, 1)),
                _full_block((1, SSM_HEADS)), _full_block((SSM_HEADS, 1)), _full_block((1, D_SSM)), _full_block((1, D_SSM)),
                state_in] + extra_specs
    if aliases is None:
        aliases = {len(in_specs) - 1: 1}
    return pl.pallas_call(
        functools.partial(_ssd_kernel, tq=tq, valid=valid),
        grid=(nb, nc),
        in_specs=in_specs,
        out_specs=[pl.BlockSpec((tq, D_SSM), lambda b, c: (b * nc + c, 0)), state],
        out_shape=[jax.ShapeDtypeStruct((rows, D_SSM), BF16), jax.ShapeDtypeStruct(state_shape, F32)],
        scratch_shapes=[pltpu.VMEM((tq + CONV_HIST, D_SSM), F32), pltpu.VMEM((tq + CONV_HIST, SSM_GN), F32),
                        pltpu.VMEM((tq + CONV_HIST, SSM_GN), F32)],
        input_output_aliases=aliases,
        compiler_params=_cparams(("arbitrary", "arbitrary")),
        name="ssd_scan",
    )(proj, proj, proj, proj, proj, cw[:, xs], cw[:, bs], cw[:, cs], cb[:, xs], cb[:, bs], cb[:, cs],
      p['ssm_dt_bias'].reshape(1, SSM_HEADS), p['ssm_dt_bias'].reshape(SSM_HEADS, 1),
      p['ssm_a_log'].reshape(1, SSM_HEADS), p['ssm_a_log'].reshape(SSM_HEADS, 1),
      jnp.repeat(p['ssm_d'], SSM_HEAD_DIM).reshape(1, D_SSM), p['ssm_norm'].reshape(1, D_SSM), h0, *extra_args)


GDN_PAIRS = GDN_HEADS // 2
GDN_PAIR_BATCH = 4
SOLVE_BLOCK = 8


def _bd(w):
    w = w.astype(BF16)
    half = w.shape[1] // 2
    z = jnp.zeros((w.shape[0], half), BF16)
    return jnp.concatenate([jnp.concatenate([w[:, :half], z], axis=1),
                            jnp.concatenate([z, w[:, half:]], axis=1)], axis=0)


def _wdot(x, y_wide):
    return jnp.dot(x.astype(BF16), _bd(y_wide), preferred_element_type=F32)


def _wdot_nt(x, y_wide):
    return lax.dot_general(x.astype(BF16), _bd(y_wide), (((1,), (1,)), ((), ())), preferred_element_type=F32)


def _blocked_unit_lower_inverse(a_list, eye_w, same_block):
    T = eye_w.shape[0]
    n = len(a_list)
    m0 = same_block(SOLVE_BLOCK)
    a_d = [jnp.where(m0, a, 0.0) for a in a_list]
    p = [eye_w - a for a in a_d]
    n_sq = _n_squarings(SOLVE_BLOCK)
    if n_sq > 0:
        pw = [_wdot(a, a) for a in a_d]
        for i in range(n_sq):
            if i + 1 < n_sq:
                r = [_wdot(jnp.concatenate([p[j], pw[j]], axis=0), pw[j]) for j in range(n)]
                p = [p[j] + r[j][:T] for j in range(n)]
                pw = [r[j][T:] for j in range(n)]
            else:
                p = [p[j] + _wdot(p[j], pw[j]) for j in range(n)]
    b = SOLVE_BLOCK
    while b < T:
        m_off = same_block(2 * b) & jnp.logical_not(same_block(b))
        t = [_wdot(p[j], jnp.where(m_off, a_list[j], 0.0)) for j in range(n)]
        p = [p[j] - _wdot(t[j], p[j]) for j in range(n)]
        b *= 2
    return p


def _n_squarings(n_valid):
    n_sq = 0
    while (1 << (n_sq + 1)) < n_valid:
        n_sq += 1
    return n_sq


def _gdn_kernel(qraw_ref, kraw_ref, vraw_ref, gate_ref, small_ref, wq_ref, wk_ref, wv_ref, alog_ref, alogt_ref,
                dtb_ref, dtbt_ref, norm_ref, s0_ref, o_ref, s_ref, bufq, bufk, bufv, q_s, k_s, v_s):
    T = T_PAD
    K = GDN_HEAD_DIM
    W = 2 * K

    @pl.when(pl.program_id(1) == 0)
    def _():
        s_ref[...] = s0_ref[...]

    row = lax.broadcasted_iota(jnp.int32, (T, T), 0)
    col = lax.broadcasted_iota(jnp.int32, (T, T), 1)
    tri_l = (row >= col).astype(F32)
    tri_u = (row <= col).astype(F32)
    row_w = lax.broadcasted_iota(jnp.int32, (T, W), 0)
    lane_w = lax.broadcasted_iota(jnp.int32, (T, W), 1)
    col_w = lane_w % K
    first = lane_w < K
    incl_w = row_w >= col_w
    strict_w = row_w > col_w
    eye_w = (row_w == col_w).astype(F32)
    same_block = lambda b: (row_w // b) == (col_w // b)

    first_chunk = pl.program_id(1) == 0
    q_s[...] = _silu(_causal_conv(qraw_ref, bufq, wq_ref, T, first_chunk))
    k_s[...] = _silu(_causal_conv(kraw_ref, bufk, wk_ref, T, first_chunk))
    v_s[...] = _silu(_causal_conv(vraw_ref, bufv, wv_ref, T, first_chunk))
    q_ref, k_ref, v_ref = q_s, k_s, v_s
    raw, raw_t = _small_and_transpose(small_ref)
    beta = jax.nn.sigmoid(raw[:, S_BETA:S_BETA + GDN_HEADS])
    g = -jnp.exp(alog_ref[...]) * jax.nn.softplus(raw[:, S_A:S_A + GDN_HEADS] + dtb_ref[...])
    gt = -jnp.exp(alogt_ref[...]) * jax.nn.softplus(raw_t[S_A:S_A + GDN_HEADS, :] + dtbt_ref[...])
    gcs = _dot_sel_l(tri_l, g)
    gcst = _dot_sel_r(gt, tri_u)
    egcs = jnp.exp(gcs)
    last = gcs[T - 1:T, :]
    kdec = jnp.exp(last - gcs)
    cd = jnp.exp(last)
    norm_w = jnp.concatenate([norm_ref[...], norm_ref[...]], axis=1)

    def widen(x, p):
        return jnp.where(first[:x.shape[0]], x[:, 2 * p:2 * p + 1], x[:, 2 * p + 1:2 * p + 2])

    def l2n(x):
        sq = x * x
        sa = jnp.sum(sq[:, :K], axis=-1, keepdims=True)
        sb = jnp.sum(sq[:, K:], axis=-1, keepdims=True)
        return x * lax.rsqrt(jnp.where(first, sa, sb) + EPS)

    for p0 in range(0, GDN_PAIRS, GDN_PAIR_BATCH):
        pairs = list(range(p0, p0 + GDN_PAIR_BATCH))
        qs, ks, decays, a_mats = [], [], [], []
        for p in pairs:
            lanes = slice(p * W, (p + 1) * W)
            qs.append(l2n(q_ref[:, lanes]) * (K ** -0.5))
            kw = l2n(k_ref[:, lanes])
            ks.append(kw)
            gcst_w = jnp.concatenate([gcst[2 * p:2 * p + 1, :], gcst[2 * p + 1:2 * p + 2, :]], axis=1)
            decay = jnp.exp(jnp.where(incl_w, widen(gcs, p) - gcst_w, -jnp.inf))
            decays.append(decay)
            a_mats.append(jnp.where(strict_w, widen(beta, p) * _wdot_nt(kw, kw) * decay, 0.0))
        minv = _blocked_unit_lower_inverse(a_mats, eye_w, same_block)
        us = [_wdot(minv[j], v_ref[:, p * W:(p + 1) * W] * widen(beta, p)) for j, p in enumerate(pairs)]
        ws = [_wdot(minv[j], ks[j] * (widen(beta, p) * widen(egcs, p))) for j, p in enumerate(pairs)]
        states = [jnp.concatenate([s_ref[0, 2 * p], s_ref[0, 2 * p + 1]], axis=1) for p in pairs]
        ws_qs = [_wdot(jnp.concatenate([ws[j], qs[j] * widen(egcs, p)], axis=0), states[j])
                 for j, p in enumerate(pairs)]
        v_new = [us[j] - ws_qs[j][:T] for j in range(len(pairs))]
        qk = [_wdot_nt(qs[j], ks[j]) * decays[j] for j in range(len(pairs))]
        for j, p in enumerate(pairs):
            o = ws_qs[j][T:] + _wdot(qk[j], v_new[j])
            kd = ks[j] * widen(kdec, p)
            kd_rows = jnp.concatenate([kd[:, :K], kd[:, K:]], axis=0).astype(BF16)
            upd = lax.dot_general(kd_rows, _bd(v_new[j]), (((0,), (0,)), ((), ())), preferred_element_type=F32)
            s_new = widen(cd, p)[:1] * states[j] + upd
            s_ref[0, 2 * p] = s_new[:, :K]
            s_ref[0, 2 * p + 1] = s_new[:, K:]
            sq = o * o
            ms = jnp.where(first, jnp.mean(sq[:, :K], axis=-1, keepdims=True), jnp.mean(sq[:, K:], axis=-1, keepdims=True))
            lanes = slice(p * W, (p + 1) * W)
            o = o * lax.rsqrt(ms + EPS) * norm_w * _silu(gate_ref[:, lanes])
            o_ref[:, lanes] = o.astype(o_ref.dtype)


def _gdn_params(p):
    cw = p['gdn_conv_w']
    return (cw[:, :D_GDN], cw[:, D_GDN:2 * D_GDN], cw[:, 2 * D_GDN:],
            p['gdn_a_log'].reshape(1, GDN_HEADS), p['gdn_a_log'].reshape(GDN_HEADS, 1),
            p['gdn_dt_bias'].reshape(1, GDN_HEADS), p['gdn_dt_bias'].reshape(GDN_HEADS, 1),
            p['gdn_norm'].reshape(1, GDN_HEAD_DIM))


def _gdn_specs(tq, nc, layer):
    return [_proj_block(tq, nc, D_GDN, P_Q), _proj_block(tq, nc, D_GDN, P_K), _proj_block(tq, nc, D_GDN, P_V),
            _proj_block(tq, nc, D_GDN, P_GATE), _proj_block(tq, nc, LANES, P_SMALL),
            _full_block((CONV_K, D_GDN)), _full_block((CONV_K, D_GDN)), _full_block((CONV_K, D_GDN)),
            _full_block((1, GDN_HEADS)), _full_block((GDN_HEADS, 1)), _full_block((1, GDN_HEADS)), _full_block((GDN_HEADS, 1)),
            _full_block((1, GDN_HEAD_DIM)),
            pl.BlockSpec((None, 1, GDN_HEADS, GDN_HEAD_DIM, GDN_HEAD_DIM), lambda b, c: (layer, b, 0, 0, 0))]


def gdn_scan(proj, p, s0, layer, *, nb, nc):
    tq = T_PAD
    rows = nb * nc * tq
    state = pl.BlockSpec((1, GDN_HEADS, GDN_HEAD_DIM, GDN_HEAD_DIM), lambda b, c: (b, 0, 0, 0))
    conv_buf = pltpu.VMEM((tq + CONV_HIST, D_GDN), F32)
    act = pltpu.VMEM((tq, D_GDN), F32)
    return pl.pallas_call(
        _gdn_kernel,
        grid=(nb, nc),
        in_specs=_gdn_specs(tq, nc, layer),
        out_specs=[pl.BlockSpec((tq, D_GDN), lambda b, c: (b * nc + c, 0)), state],
        out_shape=[jax.ShapeDtypeStruct((rows, D_GDN), BF16),
                   jax.ShapeDtypeStruct((nb, GDN_HEADS, GDN_HEAD_DIM, GDN_HEAD_DIM), F32)],
        scratch_shapes=[conv_buf, conv_buf, conv_buf, act, act, act],
        compiler_params=_cparams(("arbitrary", "arbitrary")),
        name="gdn_scan",
    )(proj, proj, proj, proj, proj, *_gdn_params(p), s0)


def _gdn_decode_kernel(qraw_ref, kraw_ref, vraw_ref, gate_ref, small_ref, wq_ref, wk_ref, wv_ref, alog_ref, alogt_ref,
                       dtb_ref, dtbt_ref, norm_ref, s0_ref, *rest, valid):
    R = DEC_ROWS
    K = GDN_HEAD_DIM
    N = GDN_HEADS * R
    o_ref, s_ref, bufq, bufk, bufv, q_s, k_s, v_s = rest[-8:]

    def stack(ref):
        return jnp.concatenate([ref[:, h * K:(h + 1) * K] for h in range(GDN_HEADS)], axis=0)

    row = lax.broadcasted_iota(jnp.int32, (N, N), 0)
    col = lax.broadcasted_iota(jnp.int32, (N, N), 1)
    same = (row // R) == (col // R)
    incl = same & (row >= col)
    strict = same & (row > col)
    eye = row == col
    eye_f = eye.astype(F32)

    def to_col(rowvec):
        return jnp.sum(jnp.where(eye, rowvec, 0.0), axis=1, keepdims=True)

    first_chunk = pl.program_id(1) == 0
    q_s[...] = _silu(_causal_conv(qraw_ref, bufq, wq_ref, R, first_chunk))
    k_s[...] = _silu(_causal_conv(kraw_ref, bufk, wk_ref, R, first_chunk))
    v_s[...] = _silu(_causal_conv(vraw_ref, bufv, wv_ref, R, first_chunk))
    q_ref, k_ref, v_ref = q_s, k_s, v_s
    _, raw_t = _small_and_transpose(small_ref)
    _, cmask = _valid_masks(valid, GDN_HEADS)
    beta_t = jnp.where(cmask, jax.nn.sigmoid(raw_t[S_BETA:S_BETA + GDN_HEADS, :]), 0.0)
    g_t = jnp.where(cmask, -jnp.exp(alogt_ref[...]) * jax.nn.softplus(raw_t[S_A:S_A + GDN_HEADS, :] + dtbt_ref[...]), 0.0)
    t_row = lax.broadcasted_iota(jnp.int32, (T_PAD, N), 0)
    t_col = lax.broadcasted_iota(jnp.int32, (T_PAD, N), 1)
    tile = (t_row == t_col % R).astype(F32)
    own = lax.broadcasted_iota(jnp.int32, (GDN_HEADS, N), 0) == lax.broadcasted_iota(jnp.int32, (GDN_HEADS, N), 1) // R
    flat = lambda m: jnp.sum(jnp.where(own, _dot_sel_r(m, tile), 0.0), axis=0, keepdims=True)
    g_row = flat(g_t)
    gb = jnp.concatenate([g_row, jnp.zeros((7, N), F32)], axis=0)
    csum = _dot_sel_r(gb, (same & (row <= col)).astype(F32))
    tot = _dot_sel_r(gb, same.astype(F32))
    gcs_row = csum[0:1, :]
    last_row = tot[0:1, :]
    gcs_col = to_col(gcs_row)
    bcol = to_col(flat(beta_t))
    egcs_col = jnp.exp(gcs_col)
    kdec_col = jnp.exp(to_col(last_row) - gcs_col)
    cd_row = jnp.exp(last_row)

    q_all = stack(q_ref)
    k_all = stack(k_ref)
    v_all = stack(v_ref)
    q_all = q_all * lax.rsqrt(jnp.sum(q_all * q_all, axis=-1, keepdims=True) + EPS) * (K ** -0.5)
    k_all = k_all * lax.rsqrt(jnp.sum(k_all * k_all, axis=-1, keepdims=True) + EPS)

    decay = jnp.exp(jnp.where(incl, gcs_col - gcs_row, -jnp.inf))
    a_mat = jnp.where(strict, bcol * _dot_nt(k_all, k_all) * decay, 0.0)
    minv = eye_f - a_mat
    n_sq = _n_squarings(valid[1] - valid[0])
    pw = a_mat
    for _ in range(n_sq):
        pw = _dot(pw, pw)
        minv = minv + _dot(minv, pw)
    u_all = _dot(minv, v_all * bcol)
    w_all = _dot(minv, k_all * (bcol * egcs_col))
    qd_all = q_all * egcs_col
    qk = _dot_nt(q_all, k_all) * decay
    kd_t = _dot_nt(eye_f[:K, :K], k_all * kdec_col).astype(BF16)

    first = lax.broadcasted_iota(jnp.int32, (K, 2 * K), 1) < K
    states, v_new, q_s = [], [], []
    for p in range(GDN_PAIRS):
        ra = slice(2 * p * R, (2 * p + 1) * R)
        rb = slice((2 * p + 1) * R, (2 * p + 2) * R)
        s_w = jnp.concatenate([s0_ref[0, 2 * p], s0_ref[0, 2 * p + 1]], axis=1)
        states.append(s_w)
        lhs = jnp.concatenate([jnp.concatenate([w_all[ra], w_all[rb]], axis=1),
                               jnp.concatenate([qd_all[ra], qd_all[rb]], axis=1)], axis=0)
        both = _wdot(lhs, s_w)
        v_new += [u_all[ra] - both[:R, :K], u_all[rb] - both[:R, K:]]
        q_s += [both[R:, :K], both[R:, K:]]
    v_new_all = jnp.concatenate(v_new, axis=0)
    o = jnp.concatenate(q_s, axis=0) + _dot(qk, v_new_all)
    ms = jnp.mean(o * o, axis=-1, keepdims=True)
    o = o * lax.rsqrt(ms + EPS) * norm_ref[...] * _silu(stack(gate_ref))
    for h in range(GDN_HEADS):
        o_ref[:, h * K:(h + 1) * K] = o[h * R:(h + 1) * R].astype(o_ref.dtype)
    zero = jnp.zeros((R, 2 * K), BF16)
    zhalf = jnp.zeros((R, K), BF16)
    for p in range(GDN_PAIRS):
        va = jnp.concatenate([v_new[2 * p].astype(BF16), zhalf], axis=1)
        vb = jnp.concatenate([zhalf, v_new[2 * p + 1].astype(BF16)], axis=1)
        rhs = jnp.concatenate([zero] * (2 * p) + [va, vb] + [zero] * (GDN_HEADS - 2 - 2 * p), axis=0)
        upd = jnp.dot(kd_t, rhs, preferred_element_type=F32)
        cd_w = jnp.where(first, cd_row[:, 2 * p * R:2 * p * R + 1], cd_row[:, (2 * p + 1) * R:(2 * p + 1) * R + 1])
        s_new = cd_w * states[p] + upd
        s_ref[0, 2 * p] = s_new[:, :K]
        s_ref[0, 2 * p + 1] = s_new[:, K:]


def gdn_decode(proj, p, s0, layer, *, nb, valid, stack=None):
    tq = DEC_ROWS
    rows = nb * tq
    state, state_shape, extra_specs, extra_args, aliases = _stacked_state(
        (GDN_HEADS, GDN_HEAD_DIM, GDN_HEAD_DIM), nb, layer, stack)
    in_specs = _gdn_specs(tq, 1, layer) + extra_specs
    if aliases is None:
        aliases = {len(in_specs) - 1: 1}
    conv_buf = pltpu.VMEM((tq + CONV_HIST, D_GDN), F32)
    act = pltpu.VMEM((tq, D_GDN), F32)
    return pl.pallas_call(
        functools.partial(_gdn_decode_kernel, valid=valid),
        grid=(nb, 1),
        in_specs=in_specs,
        out_specs=[pl.BlockSpec((tq, D_GDN), lambda b, c: (b, 0)), state],
        out_shape=[jax.ShapeDtypeStruct((rows, D_GDN), BF16), jax.ShapeDtypeStruct(state_shape, F32)],
        scratch_shapes=[conv_buf, conv_buf, conv_buf, act, act, act],
        input_output_aliases=aliases,
        compiler_params=_cparams(("arbitrary", "arbitrary")),
        name="gdn_decode",
    )(proj, proj, proj, proj, proj, *_gdn_params(p), s0, *extra_args)


DEC_TOKEN_ROW = 8


def _mixers(proj, bsz, seq, st_ssm, st_gdn, layer, p, stacks=None):
    if seq >= CHUNK:
        y, ssm_new = ssd_scan(proj, p, st_ssm, layer, nb=bsz, nc=seq // CHUNK, tq=T_PAD, valid=(0, T_PAD))
        o, gdn_new = gdn_scan(proj, p, st_gdn, layer, nb=bsz, nc=seq // CHUNK)
        return y, o, ssm_new.reshape(bsz, SSM_HEADS, SSM_HEAD_DIM, D_STATE), gdn_new
    valid = (DEC_TOKEN_ROW, DEC_TOKEN_ROW + seq)
    y, ssm_new = ssd_scan(proj, p, st_ssm, layer, nb=bsz, nc=1, tq=DEC_ROWS, valid=valid, stack=(DEPTH, stacks[0]))
    o, gdn_new = gdn_decode(proj, p, st_gdn, layer, nb=bsz, valid=valid, stack=(DEPTH, stacks[1]))
    return y, o, ssm_new, gdn_new


def _conv_cols_to_ref_order(rows):
    ssm = jnp.concatenate([rows[..., P_X:P_X + D_SSM], rows[..., P_B:P_B + SSM_GN], rows[..., P_C:P_C + SSM_GN]], axis=-1)
    return ssm, rows[..., P_Q:P_Q + GDN_CONV_DIM]


def _decode_blocks(proj_s, bsz, seq, st_ssm_conv, st_gdn_conv):
    hist = jnp.zeros((bsz, CONV_K - 1, P_COLS), F32)
    hist = hist.at[..., P_X:P_X + D_SSM].set(st_ssm_conv[..., :D_SSM])
    hist = hist.at[..., P_B:P_B + SSM_GN].set(st_ssm_conv[..., D_SSM:D_SSM + SSM_GN])
    hist = hist.at[..., P_C:P_C + SSM_GN].set(st_ssm_conv[..., D_SSM + SSM_GN:])
    hist = hist.at[..., P_Q:P_Q + GDN_CONV_DIM].set(st_gdn_conv)
    lead = jnp.zeros((bsz, DEC_TOKEN_ROW - (CONV_K - 1), P_COLS), F32)
    trail = jnp.zeros((bsz, DEC_ROWS - DEC_TOKEN_ROW - seq, P_COLS), F32)
    blocks = jnp.concatenate([lead, hist, proj_s.reshape(bsz, seq, P_COLS), trail], axis=1)
    return blocks.reshape(bsz * DEC_ROWS, P_COLS)


def _reorder_w_in(w):
    small = jnp.concatenate([w[:, SPLIT_XBC:SPLIT_DT], w[:, SPLIT_GATE:]], axis=1)
    small = jnp.pad(small, ((0, 0), (0, P_COLS - P_SMALL - small.shape[1])))
    xbc = SPLIT_Z
    return jnp.concatenate([w[:, :SPLIT_Z], w[:, xbc:xbc + D_SSM], w[:, SPLIT_DT:SPLIT_GATE],
                            w[:, xbc + D_SSM:xbc + D_SSM + 2 * SSM_GN], small], axis=1).astype(BF16)


def _dense_ffn(h, x, wg, wu, wd):
    m = h.shape[0]
    tm = m // 4
    sub = tm // 4
    nt = m // tm
    te = jnp.zeros((nt,), jnp.int32)
    tmap = jnp.arange(nt, dtype=jnp.int32)
    nv = jnp.full((nt,), tm // sub, jnp.int32)
    hid = ffn_up(h, wg, wu, te, tmap, nv, tmap * (tm // LANES), tm=tm, sub=sub, gr=sub, tn=256)
    return ffn_down(hid, wd, te, tmap, nv, x, tm=tm, sub=sub, gr=sub, tn=1024, tk=1024)


MOE_TILE = 3072
MOE_SUB = 512
MOE_GRANULE = LANES


def _moe_ffn(h, x, route, wg, wu, wd):
    m = h.shape[0]
    tm, sub, gr = MOE_TILE, MOE_SUB, MOE_GRANULE
    nt = (m * TOP_K) // tm + N_EXPERTS
    top_i = route[:, :TOP_K].astype(jnp.int32)
    probs = route[:, TOP_K:2 * TOP_K]
    e_flat = top_i.reshape(-1)
    n_assign = e_flat.shape[0]
    onehot = (e_flat[:, None] == jnp.arange(N_EXPERTS, dtype=jnp.int32)[None, :]).astype(jnp.int32)
    csum = jnp.cumsum(onehot, axis=0)
    counts = csum[-1]
    rank = jnp.sum(onehot * csum, axis=1) - 1
    padded = (counts + gr - 1) // gr * gr
    group_start = jnp.cumsum(padded) - padded
    in_pos = jnp.sum(onehot * group_start[None, :], axis=1) + rank
    n_rows = n_assign + N_EXPERTS * gr + tm
    src = jnp.zeros((n_rows,), jnp.int32).at[in_pos].set(jnp.arange(n_assign, dtype=jnp.int32) // TOP_K)
    tiles_per_e = (counts + tm - 1) // tm
    tile_end_e = jnp.cumsum(tiles_per_e)
    tile_start_e = tile_end_e - tiles_per_e
    out_pos = jnp.sum(onehot * tile_start_e[None, :], axis=1) * tm + rank
    n_used = tile_end_e[-1]
    tile_ids = jnp.arange(nt, dtype=jnp.int32)
    tmap = jnp.minimum(tile_ids, n_used - 1)
    te = jnp.minimum(jnp.sum((tile_end_e[None, :] <= tmap[:, None]).astype(jnp.int32), axis=1), N_EXPERTS - 1)
    k_in_e = tmap - tile_start_e[te]
    rows_in_tile = jnp.clip(counts[te] - k_in_e * tm, 0, tm)
    nv = jnp.where(tile_ids < n_used, (rows_in_tile + gr - 1) // gr, 0).astype(jnp.int32)
    rg = ((group_start[te] + k_in_e * tm) // LANES).astype(jnp.int32)

    xs = h.at[src].get(mode="promise_in_bounds")
    hid = ffn_up(xs, wg, wu, te, tmap, nv, rg, tm=tm, sub=sub, gr=gr, tn=256)
    ys = ffn_down(hid, wd, te, tmap, nv, tm=tm, sub=sub, gr=gr, tn=1024, tk=1024)
    out_pos = out_pos.reshape(m, TOP_K)
    out = probs[:, 0:1] * ys.at[out_pos[:, 0]].get(mode="promise_in_bounds")
    out = out + probs[:, 1:2] * ys.at[out_pos[:, 1]].get(mode="promise_in_bounds")
    return x + out


def kernel(x_prompt, x_sample, state_ssm, state_ssm_conv, state_gdn, state_gdn_conv, norm_mix, w_in, ssm_conv_w, ssm_conv_b, ssm_dt_bias, ssm_a_log, ssm_d, ssm_norm, gdn_conv_w, gdn_dt_bias, gdn_a_log, gdn_norm, w_out, norm_ffn, dense_w_gate, dense_w_up, dense_w_down, moe_router, moe_w_gate, moe_w_up, moe_w_down, norm_final):
    bp, lp, d = x_prompt.shape
    bs, ls, _ = x_sample.shape
    mp, ms = bp * lp, bs * ls
    x = jnp.concatenate([x_prompt.reshape(mp, d), x_sample.reshape(ms, d)], axis=0)
    m = mp + ms
    tm_all = m // 8

    tm_out = 1024
    zero_ssm = jnp.zeros((1, bp, D_SSM, D_STATE), F32)
    zero_gdn = jnp.zeros((1, bp, GDN_HEADS, GDN_HEAD_DIM, GDN_HEAD_DIM), F32)
    ssm_states = state_ssm.reshape(DEPTH, bs, D_SSM, D_STATE)
    dec_stacks = (None, None)
    new_p = [[], [], [], []]
    new_s = [[], [], [], []]
    for i in range(DEPTH):
        p = dict(ssm_conv_w=ssm_conv_w[i], ssm_conv_b=ssm_conv_b[i], ssm_dt_bias=ssm_dt_bias[i],
                 ssm_a_log=ssm_a_log[i], ssm_d=ssm_d[i], ssm_norm=ssm_norm[i], gdn_conv_w=gdn_conv_w[i],
                 gdn_dt_bias=gdn_dt_bias[i], gdn_a_log=gdn_a_log[i], gdn_norm=gdn_norm[i])
        h = rmsnorm(x, norm_mix[i], BF16)
        proj = matmul(h, _reorder_w_in(w_in[i]), tm=tm_all, tn=512, sub=tm_all // 2)
        y_p, o_p, ssm_p, gdn_p = _mixers(proj, bp, lp, zero_ssm, zero_gdn, 0, p)
        proj_s = proj[mp:]
        y_s, o_s, ssm_s, gdn_s = _mixers(_decode_blocks(proj_s, bs, ls, state_ssm_conv[i], state_gdn_conv[i]),
                                         bs, ls, ssm_states, state_gdn, i, p, stacks=dec_stacks)
        dec_stacks = (ssm_s, gdn_s)
        tail_p = jnp.stack([proj[b * lp + lp - (CONV_K - 1):(b + 1) * lp] for b in range(bp)])
        conv_p = _conv_cols_to_ref_order(tail_p)
        ssm_tail, gdn_tail = _conv_cols_to_ref_order(proj_s.reshape(bs, ls, P_COLS))
        conv_s = (jnp.concatenate([state_ssm_conv[i], ssm_tail], axis=1)[:, ls:],
                  jnp.concatenate([state_gdn_conv[i], gdn_tail], axis=1)[:, ls:])
        for k, (vp, vs) in enumerate(((ssm_p, None), (conv_p[0], conv_s[0]), (gdn_p, None), (conv_p[1], conv_s[1]))):
            new_p[k].append(vp)
            new_s[k].append(vs)
        tok = lambda t: jnp.pad(t.reshape(bs, DEC_ROWS, -1)[:, DEC_TOKEN_ROW:DEC_TOKEN_ROW + ls].reshape(ms, -1),
                                ((0, tm_out - ms), (0, 0)))
        x = out_proj(y_p, o_p, tok(y_s), tok(o_s), w_out[i], x, tm=tm_out, tn=512, sub=256)
        j = i // 2
        if i % 2 == 0:
            h = rmsnorm(x, norm_ffn[i], BF16)
            x = _dense_ffn(h, x, dense_w_gate[j:j + 1], dense_w_up[j:j + 1], dense_w_down[j:j + 1])
        else:
            h, route = rmsnorm_router(x, norm_ffn[i], moe_router[j])
            x = _moe_ffn(h, x, route, moe_w_gate[j], moe_w_up[j], moe_w_down[j])
    y = rmsnorm(x, norm_final, F32)
    return (y[:mp].reshape(bp, lp, d), y[mp:].reshape(bs, ls, d),
            jnp.stack(new_p[0]), jnp.stack(new_p[1]), jnp.stack(new_p[2]), jnp.stack(new_p[3]),
            dec_stacks[0].reshape(DEPTH, bs, SSM_HEADS, SSM_HEAD_DIM, D_STATE), jnp.stack(new_s[1]), dec_stacks[1], jnp.stack(new_s[3]))
```

```python
import functools

import jax
import jax.numpy as jnp
from jax import lax
from jax.experimental import pallas as pl
from jax.experimental.pallas import tpu as pltpu

F32 = jnp.float32
BF16 = jnp.bfloat16

D_MODEL = 4096
DEPTH = 2
D_SSM = 2048
D_GDN = 2048
SSM_HEAD_DIM = 64
SSM_HEADS = 32
SSM_GROUPS = 4
SSM_HPG = 8
D_STATE = 128
SSM_GN = SSM_GROUPS * D_STATE
SSM_CONV_DIM = D_SSM + 2 * SSM_GN
GDN_HEAD_DIM = 128
GDN_HEADS = 16
GDN_CONV_DIM = 3 * D_GDN
CONV_K = 4
CHUNK = 128
SPLIT_Z = D_SSM
SPLIT_XBC = SPLIT_Z + SSM_CONV_DIM
SPLIT_DT = SPLIT_XBC + SSM_HEADS
SPLIT_QKV = SPLIT_DT + GDN_CONV_DIM
SPLIT_GATE = SPLIT_QKV + D_GDN
SPLIT_BETA = SPLIT_GATE + GDN_HEADS
IN_COLS = SPLIT_BETA + GDN_HEADS
D_FF = 14336
N_EXPERTS = 8
TOP_K = 2
EPS = 1e-6

P_Z, P_X, P_Q, P_K, P_V, P_GATE = 0, 2048, 4096, 6144, 8192, 10240
P_B, P_C, P_SMALL = 12288, 12800, 13312
P_COLS = 13824
S_DT, S_BETA, S_A = 0, SSM_HEADS, SSM_HEADS + GDN_HEADS
CONV_HIST = 8

LANES = 128
T_PAD = 128
DEC_ROWS = 16
VMEM_LIMIT = 56 * 1024 * 1024


def _cparams(sem):
    return pltpu.CompilerParams(dimension_semantics=sem, vmem_limit_bytes=VMEM_LIMIT)


def _dot(a, b):
    return jnp.dot(a.astype(BF16), b.astype(BF16), preferred_element_type=F32)


def _dot_nt(a, b):
    return lax.dot_general(a.astype(BF16), b.astype(BF16), (((1,), (1,)), ((), ())),
                           preferred_element_type=F32)


def _dot_tn(a, b):
    return lax.dot_general(a.astype(BF16), b.astype(BF16), (((0,), (0,)), ((), ())),
                           preferred_element_type=F32)


def _split3(a):
    a1 = a.astype(BF16)
    r1 = a - a1.astype(F32)
    a2 = r1.astype(BF16)
    a3 = (r1 - a2.astype(F32)).astype(BF16)
    return a1, a2, a3


def _dot_sel_l(sel, a):
    a1, a2, a3 = _split3(a)
    s = sel.astype(BF16)
    d = lambda p: jnp.dot(s, p, preferred_element_type=F32)
    return (d(a3) + d(a2)) + d(a1)


def _dot_sel_r(a, sel):
    a1, a2, a3 = _split3(a)
    s = sel.astype(BF16)
    d = lambda p: jnp.dot(p, s, preferred_element_type=F32)
    return (d(a3) + d(a2)) + d(a1)


def _dot_hi(a, b):
    a1 = a.astype(BF16)
    a2 = (a - a1.astype(F32)).astype(BF16)
    b1 = b.astype(BF16)
    b2 = (b - b1.astype(F32)).astype(BF16)
    d = lambda p, q: jnp.dot(p, q, preferred_element_type=F32)
    return (d(a2, b1) + d(a1, b2)) + d(a1, b1)


def _pad_rows(v, rows):
    if v.shape[0] == rows:
        return v
    return jnp.concatenate([v, jnp.zeros((rows - v.shape[0], v.shape[1]), v.dtype)], axis=0)


def _silu(v):
    return v * jax.nn.sigmoid(v)


def _causal_conv(raw_ref, buf_ref, w_ref, tq, first_chunk):
    @pl.when(first_chunk)
    def _():
        buf_ref[0:CONV_HIST, :] = jnp.zeros((CONV_HIST, buf_ref.shape[1]), F32)

    buf_ref[CONV_HIST:CONV_HIST + tq, :] = raw_ref[...]
    w = w_ref[...]
    lo = CONV_HIST - (CONV_K - 1)
    y = w[0:1] * buf_ref[lo:lo + tq, :]
    for j in range(1, CONV_K):
        y = y + w[j:j + 1] * buf_ref[lo + j:lo + j + tq, :]
    buf_ref[0:CONV_HIST, :] = buf_ref[tq:tq + CONV_HIST, :]
    return y


def _small_and_transpose(small_ref):
    raw = _pad_rows(small_ref[...], T_PAD)
    row = lax.broadcasted_iota(jnp.int32, (T_PAD, T_PAD), 0)
    col = lax.broadcasted_iota(jnp.int32, (T_PAD, T_PAD), 1)
    eye = (row == col).astype(BF16)
    a1, a2, a3 = _split3(raw)
    d = lambda p: lax.dot_general(p, eye, (((0,), (0,)), ((), ())), preferred_element_type=F32)
    return raw, (d(a3) + d(a2)) + d(a1)


def _valid_masks(valid, width):
    lo, hi = valid
    r = lax.broadcasted_iota(jnp.int32, (T_PAD, width), 0)
    c = lax.broadcasted_iota(jnp.int32, (width, T_PAD), 1)
    return (r >= lo) & (r < hi), (c >= lo) & (c < hi)


def _rmsnorm_kernel(x_ref, g_ref, o_ref):
    x = x_ref[...]
    ms = jnp.mean(x * x, axis=-1, keepdims=True)
    o_ref[...] = (x * lax.rsqrt(ms + EPS) * g_ref[...]).astype(o_ref.dtype)


def rmsnorm(x, gain, out_dtype, tm=256):
    m, d = x.shape
    return pl.pallas_call(
        _rmsnorm_kernel,
        grid=(m // tm,),
        in_specs=[pl.BlockSpec((tm, d), lambda i: (i, 0)), pl.BlockSpec((1, d), lambda i: (0, 0))],
        out_specs=pl.BlockSpec((tm, d), lambda i: (i, 0)),
        out_shape=jax.ShapeDtypeStruct((m, d), out_dtype),
        compiler_params=_cparams(("parallel",)),
        name="rmsnorm",
    )(x, gain.reshape(1, d))


def _rmsnorm_router_kernel(x_ref, g_ref, r_ref, o_ref, route_ref):
    x = x_ref[...]
    ms = jnp.mean(x * x, axis=-1, keepdims=True)
    h = x * lax.rsqrt(ms + EPS) * g_ref[...]
    o_ref[...] = h.astype(o_ref.dtype)
    logits = _dot_hi(h, r_ref[...])
    lane = lax.broadcasted_iota(jnp.int32, logits.shape, 1)
    neg = jnp.float32(-jnp.inf)
    lm = jnp.where(lane < N_EXPERTS, logits, neg)
    m1 = jnp.max(lm, axis=-1, keepdims=True)
    i1 = jnp.min(jnp.where(lm == m1, lane, LANES), axis=-1, keepdims=True)
    lm2 = jnp.where(lane == i1, neg, lm)
    m2 = jnp.max(lm2, axis=-1, keepdims=True)
    i2 = jnp.min(jnp.where(lm2 == m2, lane, LANES), axis=-1, keepdims=True)
    e = jnp.exp(m2 - m1)
    p1 = 1.0 / (1.0 + e)
    p2 = e / (1.0 + e)
    route_ref[...] = jnp.where(lane == 0, i1.astype(F32),
                               jnp.where(lane == 1, i2.astype(F32),
                                         jnp.where(lane == 2, p1, jnp.where(lane == 3, p2, 0.0))))


def rmsnorm_router(x, gain, router, tm=256):
    m, d = x.shape
    r_pad = jnp.zeros((d, LANES), F32).at[:, :N_EXPERTS].set(router)
    return pl.pallas_call(
        _rmsnorm_router_kernel,
        grid=(m // tm,),
        in_specs=[pl.BlockSpec((tm, d), lambda i: (i, 0)), pl.BlockSpec((1, d), lambda i: (0, 0)),
                  pl.BlockSpec((d, LANES), lambda i: (0, 0))],
        out_specs=[pl.BlockSpec((tm, d), lambda i: (i, 0)), pl.BlockSpec((tm, LANES), lambda i: (i, 0))],
        out_shape=[jax.ShapeDtypeStruct((m, d), BF16), jax.ShapeDtypeStruct((m, LANES), F32)],
        compiler_params=_cparams(("parallel",)),
        name="rmsnorm_router",
    )(x, gain.reshape(1, d), r_pad)


def _matmul_kernel(*refs, sub, nsub, has_res):
    if has_res:
        x_ref, w_ref, r_ref, o_ref = refs
    else:
        x_ref, w_ref, o_ref = refs
        r_ref = None

    def body(i, carry):
        rows = pl.ds(pl.multiple_of(i * sub, sub), sub)
        acc = jnp.dot(x_ref[rows, :], w_ref[...], preferred_element_type=F32)
        if has_res:
            acc = acc + r_ref[rows, :]
        o_ref[rows, :] = acc.astype(o_ref.dtype)
        return carry

    lax.fori_loop(0, nsub, body, 0)


def matmul(x, w, res=None, *, tm, tn, sub, out_dtype=F32):
    m, k = x.shape
    n = w.shape[1]
    in_specs = [pl.BlockSpec((tm, k), lambda i, j: (i, 0)), pl.BlockSpec((k, tn), lambda i, j: (0, j))]
    args = [x, w]
    if res is not None:
        in_specs.append(pl.BlockSpec((tm, tn), lambda i, j: (i, j)))
        args.append(res)
    return pl.pallas_call(
        functools.partial(_matmul_kernel, sub=sub, nsub=tm // sub, has_res=res is not None),
        grid=(m // tm, n // tn),
        in_specs=in_specs,
        out_specs=pl.BlockSpec((tm, tn), lambda i, j: (i, j)),
        out_shape=jax.ShapeDtypeStruct((m, n), out_dtype),
        compiler_params=_cparams(("parallel", "arbitrary")),
        name="matmul",
    )(*args)


def _out_proj_kernel(yp_ref, op_ref, ys_ref, os_ref, wy_ref, wo_ref, r_ref, o_ref, *, sub, nsub, n_prompt_tiles):
    def run(y_ref, g_ref):
        def body(i, carry):
            rows = pl.ds(pl.multiple_of(i * sub, sub), sub)
            acc = jnp.dot(y_ref[rows, :], wy_ref[...].astype(BF16), preferred_element_type=F32)
            acc = acc + jnp.dot(g_ref[rows, :], wo_ref[...].astype(BF16), preferred_element_type=F32)
            o_ref[rows, :] = acc + r_ref[rows, :]
            return carry

        lax.fori_loop(0, nsub, body, 0)

    is_prompt = pl.program_id(0) < n_prompt_tiles

    @pl.when(is_prompt)
    def _():
        run(yp_ref, op_ref)

    @pl.when(jnp.logical_not(is_prompt))
    def _():
        run(ys_ref, os_ref)


def out_proj(y_p, o_p, y_s, o_s, w, res, *, tm, tn, sub):
    mp, k = y_p.shape
    m, n = res.shape
    npt = mp // tm
    nt = npt + y_s.shape[0] // tm
    p_blk = pl.BlockSpec((tm, k), lambda i, j: (jnp.minimum(i, npt - 1), 0))
    s_blk = pl.BlockSpec((tm, k), lambda i, j: (jnp.maximum(i - npt, 0), 0))
    return pl.pallas_call(
        functools.partial(_out_proj_kernel, sub=sub, nsub=tm // sub, n_prompt_tiles=npt),
        grid=(nt, n // tn),
        in_specs=[p_blk, p_blk, s_blk, s_blk,
                  pl.BlockSpec((k, tn), lambda i, j: (0, j)), pl.BlockSpec((k, tn), lambda i, j: (1, j)),
                  pl.BlockSpec((tm, tn), lambda i, j: (i, j))],
        out_specs=pl.BlockSpec((tm, tn), lambda i, j: (i, j)),
        out_shape=jax.ShapeDtypeStruct((m, n), F32),
        compiler_params=_cparams(("parallel", "arbitrary")),
        name="out_proj",
    )(y_p, o_p, y_s, o_s, w, w, res)


def _row_blocks(nv, sub, gr, make):
    sizes = [sub]
    if sub // 2 > gr and (sub // 2) % gr == 0:
        sizes.append(sub // 2)
    if gr < sub:
        sizes.append(gr)
    done = 0
    for size in sizes:
        per = size // gr
        n = (nv - done) // per
        run = make(size)

        def body(i, carry, run=run, size=size, start=done):
            run(pl.multiple_of(start * gr + i * size, gr))
            return carry

        lax.fori_loop(0, n, body, 0)
        done = done + n * per


def _ffn_up_kernel(te_ref, tmap_ref, nv_ref, rg_ref, x_ref, wg_ref, wu_ref, o_ref, *, sub, gr, ngr):
    nv = nv_ref[pl.program_id(0)]

    @pl.when(nv > 0)
    def _():
        def block(size):
            def run(r0):
                rows = pl.ds(r0, size)
                xs = x_ref[rows, :]
                g = jnp.dot(xs, wg_ref[0].astype(BF16), preferred_element_type=F32)
                u = jnp.dot(xs, wu_ref[0].astype(BF16), preferred_element_type=F32)
                o_ref[rows, :] = (_silu(g) * u).astype(o_ref.dtype)
            return run

        _row_blocks(nv, sub, gr, block)

        def zero(i, carry):
            rows = pl.ds(pl.multiple_of(i * gr, gr), gr)
            o_ref[rows, :] = jnp.zeros((gr, o_ref.shape[1]), o_ref.dtype)
            return carry

        lax.fori_loop(nv, ngr, zero, 0)


def ffn_up(x, wg, wu, tile_e, tile_map, tile_nv, tile_rg, *, tm, sub, gr, tn):
    d = x.shape[1]
    nt = tile_e.shape[0]
    f = wg.shape[2]
    nj = f // tn

    def wmap(t, j, te, tmap, nv, rg):
        return (te[t], 0, jnp.where(nv[t] > 0, j, nj - 1))

    return pl.pallas_call(
        functools.partial(_ffn_up_kernel, sub=sub, gr=gr, ngr=tm // gr),
        grid_spec=pltpu.PrefetchScalarGridSpec(
            num_scalar_prefetch=4,
            grid=(nt, nj),
            in_specs=[pl.BlockSpec((pl.Element(tm), pl.Element(d)), lambda t, j, te, tmap, nv, rg: (rg[t] * LANES, 0),
                                   pipeline_mode=pl.Buffered(1)),
                      pl.BlockSpec((1, d, tn), wmap),
                      pl.BlockSpec((1, d, tn), wmap)],
            out_specs=pl.BlockSpec((tm, tn), lambda t, j, te, tmap, nv, rg: (tmap[t], jnp.where(nv[t] > 0, j, nj - 1))),
        ),
        out_shape=jax.ShapeDtypeStruct((nt * tm, f), BF16),
        compiler_params=_cparams(("arbitrary", "arbitrary")),
        name="ffn_up",
    )(tile_e, tile_map, tile_nv, tile_rg, x, wg, wu)


def _ffn_down_kernel(te_ref, tmap_ref, nv_ref, x_ref, w_ref, *rest, sub, gr, has_res):
    if has_res:
        r_ref, o_ref = rest
    else:
        (o_ref,) = rest
    k = pl.program_id(2)
    nv = nv_ref[pl.program_id(0)]

    @pl.when(nv > 0)
    def _():
        @pl.when(k == 0)
        def _():
            o_ref[...] = r_ref[...] if has_res else jnp.zeros(o_ref.shape, o_ref.dtype)

        def block(size):
            def run(r0):
                rows = pl.ds(r0, size)
                o_ref[rows, :] += jnp.dot(x_ref[rows, :], w_ref[0].astype(BF16), preferred_element_type=F32)
            return run

        _row_blocks(nv, sub, gr, block)


def ffn_down(x, wd, tile_e, tile_map, tile_nv, res=None, *, tm, sub, gr, tn, tk):
    rows, f = x.shape
    nt = rows // tm
    d = wd.shape[2]
    nj, nk = d // tn, f // tk

    def live(t, idx, last, nv):
        return jnp.where(nv[t] > 0, idx, last)

    omap = lambda t, j, k, te, tmap, nv: (tmap[t], live(t, j, nj - 1, nv))
    in_specs = [pl.BlockSpec((tm, tk), lambda t, j, k, te, tmap, nv: (tmap[t], live(t, k, nk - 1, nv))),
                pl.BlockSpec((1, tk, tn), lambda t, j, k, te, tmap, nv:
                             (te[t], live(t, k, nk - 1, nv), live(t, j, nj - 1, nv)))]
    args = [tile_e, tile_map, tile_nv, x, wd]
    if res is not None:
        in_specs.append(pl.BlockSpec((tm, tn), omap))
        args.append(res)
    return pl.pallas_call(
        functools.partial(_ffn_down_kernel, sub=sub, gr=gr, has_res=res is not None),
        grid_spec=pltpu.PrefetchScalarGridSpec(
            num_scalar_prefetch=3,
            grid=(nt, nj, nk),
            in_specs=in_specs,
            out_specs=pl.BlockSpec((tm, tn), omap),
        ),
        out_shape=jax.ShapeDtypeStruct((rows, d), F32),
        compiler_params=_cparams(("arbitrary", "arbitrary", "arbitrary")),
        name="ffn_down",
    )(*args)


def _ssd_kernel(z_ref, xraw_ref, braw_ref, craw_ref, small_ref, wx_ref, wb_ref, wc_ref, bx_ref, bb_ref, bc_ref,
                dtb_ref, dtbt_ref, alog_ref, alogt_ref, dexp_ref, norm_ref, h0_ref, *rest, tq, valid):
    T = T_PAD
    y_ref, h_ref, bufx, bufb, bufc = rest[-5:]

    @pl.when(pl.program_id(1) == 0)
    def _():
        h_ref[...] = h0_ref[...]

    row = lax.broadcasted_iota(jnp.int32, (T, T), 0)
    col = lax.broadcasted_iota(jnp.int32, (T, T), 1)
    incl = (row >= col)
    tri_l = incl.astype(F32)
    tri_u = (row <= col).astype(F32)

    first_chunk = pl.program_id(1) == 0
    x_act = _silu(_causal_conv(xraw_ref, bufx, wx_ref, tq, first_chunk) + bx_ref[...])
    b_act = _silu(_causal_conv(braw_ref, bufb, wb_ref, tq, first_chunk) + bb_ref[...])
    c_act = _silu(_causal_conv(craw_ref, bufc, wc_ref, tq, first_chunk) + bc_ref[...])
    raw, raw_t = _small_and_transpose(small_ref)
    rmask, cmask = _valid_masks(valid, SSM_HEADS)
    dt = jnp.where(rmask, jax.nn.softplus(raw[:, S_DT:S_DT + SSM_HEADS] + dtb_ref[...]), 0.0)
    dtt = jnp.where(cmask, jax.nn.softplus(raw_t[S_DT:S_DT + SSM_HEADS, :] + dtbt_ref[...]), 0.0)
    a = dt * (-jnp.exp(alog_ref[...]))
    at = dtt * (-jnp.exp(alogt_ref[...]))
    acs = _dot_sel_l(tri_l, a)
    acst = _dot_sel_r(at, tri_u)
    last = acs[T - 1:T, :]
    e_last = jnp.exp(last)
    wts = jnp.exp(last - acs) * dt

    hp = lax.broadcasted_iota(jnp.int32, (SSM_HEADS, D_SSM), 1) // SSM_HEAD_DIM
    expand = (hp == lax.broadcasted_iota(jnp.int32, (SSM_HEADS, D_SSM), 0)).astype(F32)
    expand_b = expand.astype(BF16)
    lane_expand = lambda v: jnp.dot(v.astype(BF16), expand_b, preferred_element_type=F32)
    eacs_e = lane_expand(jnp.exp(acs[:tq]))
    xdt = _pad_rows((x_act * lane_expand(dt[:tq])).astype(BF16), T)
    xw = _pad_rows((x_act * lane_expand(wts[:tq])).astype(BF16), T)
    incl_q = incl[:tq]
    half = lax.broadcasted_iota(jnp.int32, (tq, LANES), 1) < SSM_HEAD_DIM
    gw = SSM_HPG * SSM_HEAD_DIM

    y_groups = []
    for g in range(SSM_GROUPS):
        bg = _pad_rows(b_act[:, g * D_STATE:(g + 1) * D_STATE], T).astype(BF16)
        cg = c_act[:, g * D_STATE:(g + 1) * D_STATE].astype(BF16)
        cb = _dot_nt(cg, bg)
        hg = h_ref[0, g * gw:(g + 1) * gw, :]
        y_state = _dot_nt(cg, hg)
        pieces = []
        for j in range(SSM_HPG // 2):
            ys = []
            for hh in (2 * j, 2 * j + 1):
                h = g * SSM_HPG + hh
                seg = acs[:tq, h:h + 1] - acst[h:h + 1, :]
                w = cb * jnp.exp(jnp.where(incl_q, seg, -jnp.inf))
                lo = g * gw + j * LANES
                ys.append(jnp.dot(w.astype(BF16), xdt[:, lo:lo + LANES], preferred_element_type=F32))
            pieces.append(jnp.where(half, ys[0], ys[1]))
        y_intra = jnp.concatenate(pieces, axis=1)
        y_groups.append(y_intra + eacs_e[:, g * gw:(g + 1) * gw] * y_state)
        upd = _dot_tn(xw[:, g * gw:(g + 1) * gw], bg)
        for hh in range(SSM_HPG):
            h = g * SSM_HPG + hh
            r0 = g * gw + hh * SSM_HEAD_DIM
            h_ref[0, r0:r0 + SSM_HEAD_DIM, :] = (e_last[:, h:h + 1] * hg[hh * SSM_HEAD_DIM:(hh + 1) * SSM_HEAD_DIM, :]
                                                 + upd[hh * SSM_HEAD_DIM:(hh + 1) * SSM_HEAD_DIM, :])

    y = jnp.concatenate(y_groups, axis=1)
    y = y + dexp_ref[...] * x_act
    y = y * _silu(z_ref[...])
    outs = []
    for g in range(SSM_GROUPS):
        yg = y[:, g * gw:(g + 1) * gw]
        ms = jnp.mean(yg * yg, axis=-1, keepdims=True)
        outs.append(yg * lax.rsqrt(ms + EPS))
    y_ref[...] = (jnp.concatenate(outs, axis=1) * norm_ref[...]).astype(y_ref.dtype)


def _proj_block(tq, nc, width, offset):
    return pl.BlockSpec((tq, width), lambda b, c: (b * nc + c, offset // width))


def _full_block(shape):
    return pl.BlockSpec(shape, lambda b, c: tuple(0 for _ in shape))


def _stacked_state(shape, nb, layer, stack):
    zeros = (0,) * len(shape)
    if stack is None:
        return pl.BlockSpec((1,) + shape, lambda b, c: (b,) + zeros), (nb,) + shape, [], [], {}
    depth, prev = stack
    spec = pl.BlockSpec((None, 1) + shape, lambda b, c: (layer, b) + zeros)
    if prev is None:
        return spec, (depth, nb) + shape, [], [], {}
    return spec, (depth, nb) + shape, [pl.BlockSpec(memory_space=pl.ANY)], [prev], None


def ssd_scan(proj, p, h0, layer, *, nb, nc, tq, valid, stack=None):
    rows = nb * nc * tq
    cw, cb = p['ssm_conv_w'], p['ssm_conv_b'].reshape(1, SSM_CONV_DIM)
    xs, bs, cs = slice(0, D_SSM), slice(D_SSM, D_SSM + SSM_GN), slice(D_SSM + SSM_GN, SSM_CONV_DIM)
    state_in = pl.BlockSpec((None, 1, D_SSM, D_STATE), lambda b, c: (layer, b, 0, 0))
    state, state_shape, extra_specs, extra_args, aliases = _stacked_state((D_SSM, D_STATE), nb, layer, stack)
    in_specs = [_proj_block(tq, nc, D_SSM, P_Z), _proj_block(tq, nc, D_SSM, P_X), _proj_block(tq, nc, SSM_GN, P_B),
                _proj_block(tq, nc, SSM_GN, P_C), _proj_block(tq, nc, LANES, P_SMALL),
                _full_block((CONV_K, D_SSM)), _full_block((CONV_K, SSM_GN)), _full_block((CONV_K, SSM_GN)),
                _full_block((1, D_SSM)), _full_block((1, SSM_GN)), _full_block((1, SSM_GN)),
                _full_block((1, SSM_HEADS)), _full_block((SSM_HEADS, 1)),
                _full_block((1, SSM_HEADS)), _full_block((SSM_HEADS, 1)), _full_block((1, D_SSM)), _full_block((1, D_SSM)),
                state_in] + extra_specs
    if aliases is None:
        aliases = {len(in_specs) - 1: 1}
    return pl.pallas_call(
        functools.partial(_ssd_kernel, tq=tq, valid=valid),
        grid=(nb, nc),
        in_specs=in_specs,
        out_specs=[pl.BlockSpec((tq, D_SSM), lambda b, c: (b * nc + c, 0)), state],
        out_shape=[jax.ShapeDtypeStruct((rows, D_SSM), BF16), jax.ShapeDtypeStruct(state_shape, F32)],
        scratch_shapes=[pltpu.VMEM((tq + CONV_HIST, D_SSM), F32), pltpu.VMEM((tq + CONV_HIST, SSM_GN), F32),
                        pltpu.VMEM((tq + CONV_HIST, SSM_GN), F32)],
        input_output_aliases=aliases,
        compiler_params=_cparams(("arbitrary", "arbitrary")),
        name="ssd_scan",
    )(proj, proj, proj, proj, proj, cw[:, xs], cw[:, bs], cw[:, cs], cb[:, xs], cb[:, bs], cb[:, cs],
      p['ssm_dt_bias'].reshape(1, SSM_HEADS), p['ssm_dt_bias'].reshape(SSM_HEADS, 1),
      p['ssm_a_log'].reshape(1, SSM_HEADS), p['ssm_a_log'].reshape(SSM_HEADS, 1),
      jnp.repeat(p['ssm_d'], SSM_HEAD_DIM).reshape(1, D_SSM), p['ssm_norm'].reshape(1, D_SSM), h0, *extra_args)


GDN_PAIRS = GDN_HEADS // 2
GDN_PAIR_BATCH = 4
SOLVE_BLOCK = 8


def _bd(w):
    w = w.astype(BF16)
    half = w.shape[1] // 2
    z = jnp.zeros((w.shape[0], half), BF16)
    return jnp.concatenate([jnp.concatenate([w[:, :half], z], axis=1),
                            jnp.concatenate([z, w[:, half:]], axis=1)], axis=0)


def _wdot(x, y_wide):
    return jnp.dot(x.astype(BF16), _bd(y_wide), preferred_element_type=F32)


def _wdot_nt(x, y_wide):
    return lax.dot_general(x.astype(BF16), _bd(y_wide), (((1,), (1,)), ((), ())), preferred_element_type=F32)


def _blocked_unit_lower_inverse(a_list, eye_w, same_block):
    T = eye_w.shape[0]
    n = len(a_list)
    m0 = same_block(SOLVE_BLOCK)
    a_d = [jnp.where(m0, a, 0.0) for a in a_list]
    p = [eye_w - a for a in a_d]
    n_sq = _n_squarings(SOLVE_BLOCK)
    if n_sq > 0:
        pw = [_wdot(a, a) for a in a_d]
        for i in range(n_sq):
            if i + 1 < n_sq:
                r = [_wdot(jnp.concatenate([p[j], pw[j]], axis=0), pw[j]) for j in range(n)]
                p = [p[j] + r[j][:T] for j in range(n)]
                pw = [r[j][T:] for j in range(n)]
            else:
                p = [p[j] + _wdot(p[j], pw[j]) for j in range(n)]
    b = SOLVE_BLOCK
    while b < T:
        m_off = same_block(2 * b) & jnp.logical_not(same_block(b))
        t = [_wdot(p[j], jnp.where(m_off, a_list[j], 0.0)) for j in range(n)]
        p = [p[j] - _wdot(t[j], p[j]) for j in range(n)]
        b *= 2
    return p


def _n_squarings(n_valid):
    n_sq = 0
    while (1 << (n_sq + 1)) < n_valid:
        n_sq += 1
    return n_sq


def _gdn_kernel(qraw_ref, kraw_ref, vraw_ref, gate_ref, small_ref, wq_ref, wk_ref, wv_ref, alog_ref, alogt_ref,
                dtb_ref, dtbt_ref, norm_ref, s0_ref, o_ref, s_ref, bufq, bufk, bufv, q_s, k_s, v_s):
    T = T_PAD
    K = GDN_HEAD_DIM
    W = 2 * K

    @pl.when(pl.program_id(1) == 0)
    def _():
        s_ref[...] = s0_ref[...]

    row = lax.broadcasted_iota(jnp.int32, (T, T), 0)
    col = lax.broadcasted_iota(jnp.int32, (T, T), 1)
    tri_l = (row >= col).astype(F32)
    tri_u = (row <= col).astype(F32)
    row_w = lax.broadcasted_iota(jnp.int32, (T, W), 0)
    lane_w = lax.broadcasted_iota(jnp.int32, (T, W), 1)
    col_w = lane_w % K
    first = lane_w < K
    incl_w = row_w >= col_w
    strict_w = row_w > col_w
    eye_w = (row_w == col_w).astype(F32)
    same_block = lambda b: (row_w // b) == (col_w // b)

    first_chunk = pl.program_id(1) == 0
    q_s[...] = _silu(_causal_conv(qraw_ref, bufq, wq_ref, T, first_chunk))
    k_s[...] = _silu(_causal_conv(kraw_ref, bufk, wk_ref, T, first_chunk))
    v_s[...] = _silu(_causal_conv(vraw_ref, bufv, wv_ref, T, first_chunk))
    q_ref, k_ref, v_ref = q_s, k_s, v_s
    raw, raw_t = _small_and_transpose(small_ref)
    beta = jax.nn.sigmoid(raw[:, S_BETA:S_BETA + GDN_HEADS])
    g = -jnp.exp(alog_ref[...]) * jax.nn.softplus(raw[:, S_A:S_A + GDN_HEADS] + dtb_ref[...])
    gt = -jnp.exp(alogt_ref[...]) * jax.nn.softplus(raw_t[S_A:S_A + GDN_HEADS, :] + dtbt_ref[...])
    gcs = _dot_sel_l(tri_l, g)
    gcst = _dot_sel_r(gt, tri_u)
    egcs = jnp.exp(gcs)
    last = gcs[T - 1:T, :]
    kdec = jnp.exp(last - gcs)
    cd = jnp.exp(last)
    norm_w = jnp.concatenate([norm_ref[...], norm_ref[...]], axis=1)

    def widen(x, p):
        return jnp.where(first[:x.shape[0]], x[:, 2 * p:2 * p + 1], x[:, 2 * p + 1:2 * p + 2])

    def l2n(x):
        sq = x * x
        sa = jnp.sum(sq[:, :K], axis=-1, keepdims=True)
        sb = jnp.sum(sq[:, K:], axis=-1, keepdims=True)
        return x * lax.rsqrt(jnp.where(first, sa, sb) + EPS)

    for p0 in range(0, GDN_PAIRS, GDN_PAIR_BATCH):
        pairs = list(range(p0, p0 + GDN_PAIR_BATCH))
        qs, ks, decays, a_mats = [], [], [], []
        for p in pairs:
            lanes = slice(p * W, (p + 1) * W)
            qs.append(l2n(q_ref[:, lanes]) * (K ** -0.5))
            kw = l2n(k_ref[:, lanes])
            ks.append(kw)
            gcst_w = jnp.concatenate([gcst[2 * p:2 * p + 1, :], gcst[2 * p + 1:2 * p + 2, :]], axis=1)
            decay = jnp.exp(jnp.where(incl_w, widen(gcs, p) - gcst_w, -jnp.inf))
            decays.append(decay)
            a_mats.append(jnp.where(strict_w, widen(beta, p) * _wdot_nt(kw, kw) * decay, 0.0))
        minv = _blocked_unit_lower_inverse(a_mats, eye_w, same_block)
        us = [_wdot(minv[j], v_ref[:, p * W:(p + 1) * W] * widen(beta, p)) for j, p in enumerate(pairs)]
        ws = [_wdot(minv[j], ks[j] * (widen(beta, p) * widen(egcs, p))) for j, p in enumerate(pairs)]
        states = [jnp.concatenate([s_ref[0, 2 * p], s_ref[0, 2 * p + 1]], axis=1) for p in pairs]
        ws_qs = [_wdot(jnp.concatenate([ws[j], qs[j] * widen(egcs, p)], axis=0), states[j])
                 for j, p in enumerate(pairs)]
        v_new = [us[j] - ws_qs[j][:T] for j in range(len(pairs))]
        qk = [_wdot_nt(qs[j], ks[j]) * decays[j] for j in range(len(pairs))]
        for j, p in enumerate(pairs):
            o = ws_qs[j][T:] + _wdot(qk[j], v_new[j])
            kd = ks[j] * widen(kdec, p)
            kd_rows = jnp.concatenate([kd[:, :K], kd[:, K:]], axis=0).astype(BF16)
            upd = lax.dot_general(kd_rows, _bd(v_new[j]), (((0,), (0,)), ((), ())), preferred_element_type=F32)
            s_new = widen(cd, p)[:1] * states[j] + upd
            s_ref[0, 2 * p] = s_new[:, :K]
            s_ref[0, 2 * p + 1] = s_new[:, K:]
            sq = o * o
            ms = jnp.where(first, jnp.mean(sq[:, :K], axis=-1, keepdims=True), jnp.mean(sq[:, K:], axis=-1, keepdims=True))
            lanes = slice(p * W, (p + 1) * W)
            o = o * lax.rsqrt(ms + EPS) * norm_w * _silu(gate_ref[:, lanes])
            o_ref[:, lanes] = o.astype(o_ref.dtype)


def _gdn_params(p):
    cw = p['gdn_conv_w']
    return (cw[:, :D_GDN], cw[:, D_GDN:2 * D_GDN], cw[:, 2 * D_GDN:],
            p['gdn_a_log'].reshape(1, GDN_HEADS), p['gdn_a_log'].reshape(GDN_HEADS, 1),
            p['gdn_dt_bias'].reshape(1, GDN_HEADS), p['gdn_dt_bias'].reshape(GDN_HEADS, 1),
            p['gdn_norm'].reshape(1, GDN_HEAD_DIM))


def _gdn_specs(tq, nc, layer):
    return [_proj_block(tq, nc, D_GDN, P_Q), _proj_block(tq, nc, D_GDN, P_K), _proj_block(tq, nc, D_GDN, P_V),
            _proj_block(tq, nc, D_GDN, P_GATE), _proj_block(tq, nc, LANES, P_SMALL),
            _full_block((CONV_K, D_GDN)), _full_block((CONV_K, D_GDN)), _full_block((CONV_K, D_GDN)),
            _full_block((1, GDN_HEADS)), _full_block((GDN_HEADS, 1)), _full_block((1, GDN_HEADS)), _full_block((GDN_HEADS, 1)),
            _full_block((1, GDN_HEAD_DIM)),
            pl.BlockSpec((None, 1, GDN_HEADS, GDN_HEAD_DIM, GDN_HEAD_DIM), lambda b, c: (layer, b, 0, 0, 0))]


def gdn_scan(proj, p, s0, layer, *, nb, nc):
    tq = T_PAD
    rows = nb * nc * tq
    state = pl.BlockSpec((1, GDN_HEADS, GDN_HEAD_DIM, GDN_HEAD_DIM), lambda b, c: (b, 0, 0, 0))
    conv_buf = pltpu.VMEM((tq + CONV_HIST, D_GDN), F32)
    act = pltpu.VMEM((tq, D_GDN), F32)
    return pl.pallas_call(
        _gdn_kernel,
        grid=(nb, nc),
        in_specs=_gdn_specs(tq, nc, layer),
        out_specs=[pl.BlockSpec((tq, D_GDN), lambda b, c: (b * nc + c, 0)), state],
        out_shape=[jax.ShapeDtypeStruct((rows, D_GDN), BF16),
                   jax.ShapeDtypeStruct((nb, GDN_HEADS, GDN_HEAD_DIM, GDN_HEAD_DIM), F32)],
        scratch_shapes=[conv_buf, conv_buf, conv_buf, act, act, act],
        compiler_params=_cparams(("arbitrary", "arbitrary")),
        name="gdn_scan",
    )(proj, proj, proj, proj, proj, *_gdn_params(p), s0)


def _gdn_decode_kernel(qraw_ref, kraw_ref, vraw_ref, gate_ref, small_ref, wq_ref, wk_ref, wv_ref, alog_ref, alogt_ref,
                       dtb_ref, dtbt_ref, norm_ref, s0_ref, *rest, valid):
    R = DEC_ROWS
    K = GDN_HEAD_DIM
    N = GDN_HEADS * R
    o_ref, s_ref, bufq, bufk, bufv, q_s, k_s, v_s = rest[-8:]

    def stack(ref):
        return jnp.concatenate([ref[:, h * K:(h + 1) * K] for h in range(GDN_HEADS)], axis=0)

    row = lax.broadcasted_iota(jnp.int32, (N, N), 0)
    col = lax.broadcasted_iota(jnp.int32, (N, N), 1)
    same = (row // R) == (col // R)
    incl = same & (row >= col)
    strict = same & (row > col)
    eye = row == col
    eye_f = eye.astype(F32)

    def to_col(rowvec):
        return jnp.sum(jnp.where(eye, rowvec, 0.0), axis=1, keepdims=True)

    first_chunk = pl.program_id(1) == 0
    q_s[...] = _silu(_causal_conv(qraw_ref, bufq, wq_ref, R, first_chunk))
    k_s[...] = _silu(_causal_conv(kraw_ref, bufk, wk_ref, R, first_chunk))
    v_s[...] = _silu(_causal_conv(vraw_ref, bufv, wv_ref, R, first_chunk))
    q_ref, k_ref, v_ref = q_s, k_s, v_s
    _, raw_t = _small_and_transpose(small_ref)
    _, cmask = _valid_masks(valid, GDN_HEADS)
    beta_t = jnp.where(cmask, jax.nn.sigmoid(raw_t[S_BETA:S_BETA + GDN_HEADS, :]), 0.0)
    g_t = jnp.where(cmask, -jnp.exp(alogt_ref[...]) * jax.nn.softplus(raw_t[S_A:S_A + GDN_HEADS, :] + dtbt_ref[...]), 0.0)
    t_row = lax.broadcasted_iota(jnp.int32, (T_PAD, N), 0)
    t_col = lax.broadcasted_iota(jnp.int32, (T_PAD, N), 1)
    tile = (t_row == t_col % R).astype(F32)
    own = lax.broadcasted_iota(jnp.int32, (GDN_HEADS, N), 0) == lax.broadcasted_iota(jnp.int32, (GDN_HEADS, N), 1) // R
    flat = lambda m: jnp.sum(jnp.where(own, _dot_sel_r(m, tile), 0.0), axis=0, keepdims=True)
    g_row = flat(g_t)
    gb = jnp.concatenate([g_row, jnp.zeros((7, N), F32)], axis=0)
    csum = _dot_sel_r(gb, (same & (row <= col)).astype(F32))
    tot = _dot_sel_r(gb, same.astype(F32))
    gcs_row = csum[0:1, :]
    last_row = tot[0:1, :]
    gcs_col = to_col(gcs_row)
    bcol = to_col(flat(beta_t))
    egcs_col = jnp.exp(gcs_col)
    kdec_col = jnp.exp(to_col(last_row) - gcs_col)
    cd_row = jnp.exp(last_row)

    q_all = stack(q_ref)
    k_all = stack(k_ref)
    v_all = stack(v_ref)
    q_all = q_all * lax.rsqrt(jnp.sum(q_all * q_all, axis=-1, keepdims=True) + EPS) * (K ** -0.5)
    k_all = k_all * lax.rsqrt(jnp.sum(k_all * k_all, axis=-1, keepdims=True) + EPS)

    decay = jnp.exp(jnp.where(incl, gcs_col - gcs_row, -jnp.inf))
    a_mat = jnp.where(strict, bcol * _dot_nt(k_all, k_all) * decay, 0.0)
    minv = eye_f - a_mat
    n_sq = _n_squarings(valid[1] - valid[0])
    pw = a_mat
    for _ in range(n_sq):
        pw = _dot(pw, pw)
        minv = minv + _dot(minv, pw)
    u_all = _dot(minv, v_all * bcol)
    w_all = _dot(minv, k_all * (bcol * egcs_col))
    qd_all = q_all * egcs_col
    qk = _dot_nt(q_all, k_all) * decay
    kd_t = _dot_nt(eye_f[:K, :K], k_all * kdec_col).astype(BF16)

    first = lax.broadcasted_iota(jnp.int32, (K, 2 * K), 1) < K
    states, v_new, q_s = [], [], []
    for p in range(GDN_PAIRS):
        ra = slice(2 * p * R, (2 * p + 1) * R)
        rb = slice((2 * p + 1) * R, (2 * p + 2) * R)
        s_w = jnp.concatenate([s0_ref[0, 2 * p], s0_ref[0, 2 * p + 1]], axis=1)
        states.append(s_w)
        lhs = jnp.concatenate([jnp.concatenate([w_all[ra], w_all[rb]], axis=1),
                               jnp.concatenate([qd_all[ra], qd_all[rb]], axis=1)], axis=0)
        both = _wdot(lhs, s_w)
        v_new += [u_all[ra] - both[:R, :K], u_all[rb] - both[:R, K:]]
        q_s += [both[R:, :K], both[R:, K:]]
    v_new_all = jnp.concatenate(v_new, axis=0)
    o = jnp.concatenate(q_s, axis=0) + _dot(qk, v_new_all)
    ms = jnp.mean(o * o, axis=-1, keepdims=True)
    o = o * lax.rsqrt(ms + EPS) * norm_ref[...] * _silu(stack(gate_ref))
    for h in range(GDN_HEADS):
        o_ref[:, h * K:(h + 1) * K] = o[h * R:(h + 1) * R].astype(o_ref.dtype)
    zero = jnp.zeros((R, 2 * K), BF16)
    zhalf = jnp.zeros((R, K), BF16)
    for p in range(GDN_PAIRS):
        va = jnp.concatenate([v_new[2 * p].astype(BF16), zhalf], axis=1)
        vb = jnp.concatenate([zhalf, v_new[2 * p + 1].astype(BF16)], axis=1)
        rhs = jnp.concatenate([zero] * (2 * p) + [va, vb] + [zero] * (GDN_HEADS - 2 - 2 * p), axis=0)
        upd = jnp.dot(kd_t, rhs, preferred_element_type=F32)
        cd_w = jnp.where(first, cd_row[:, 2 * p * R:2 * p * R + 1], cd_row[:, (2 * p + 1) * R:(2 * p + 1) * R + 1])
        s_new = cd_w * states[p] + upd
        s_ref[0, 2 * p] = s_new[:, :K]
        s_ref[0, 2 * p + 1] = s_new[:, K:]


def gdn_decode(proj, p, s0, layer, *, nb, valid, stack=None):
    tq = DEC_ROWS
    rows = nb * tq
    state, state_shape, extra_specs, extra_args, aliases = _stacked_state(
        (GDN_HEADS, GDN_HEAD_DIM, GDN_HEAD_DIM), nb, layer, stack)
    in_specs = _gdn_specs(tq, 1, layer) + extra_specs
    if aliases is None:
        aliases = {len(in_specs) - 1: 1}
    conv_buf = pltpu.VMEM((tq + CONV_HIST, D_GDN), F32)
    act = pltpu.VMEM((tq, D_GDN), F32)
    return pl.pallas_call(
        functools.partial(_gdn_decode_kernel, valid=valid),
        grid=(nb, 1),
        in_specs=in_specs,
        out_specs=[pl.BlockSpec((tq, D_GDN), lambda b, c: (b, 0)), state],
        out_shape=[jax.ShapeDtypeStruct((rows, D_GDN), BF16), jax.ShapeDtypeStruct(state_shape, F32)],
        scratch_shapes=[conv_buf, conv_buf, conv_buf, act, act, act],
        input_output_aliases=aliases,
        compiler_params=_cparams(("arbitrary", "arbitrary")),
        name="gdn_decode",
    )(proj, proj, proj, proj, proj, *_gdn_params(p), s0, *extra_args)


DEC_TOKEN_ROW = 8


def _mixers(proj, bsz, seq, st_ssm, st_gdn, layer, p, stacks=None):
    if seq >= CHUNK:
        y, ssm_new = ssd_scan(proj, p, st_ssm, layer, nb=bsz, nc=seq // CHUNK, tq=T_PAD, valid=(0, T_PAD))
        o, gdn_new = gdn_scan(proj, p, st_gdn, layer, nb=bsz, nc=seq // CHUNK)
        return y, o, ssm_new.reshape(bsz, SSM_HEADS, SSM_HEAD_DIM, D_STATE), gdn_new
    valid = (DEC_TOKEN_ROW, DEC_TOKEN_ROW + seq)
    y, ssm_new = ssd_scan(proj, p, st_ssm, layer, nb=bsz, nc=1, tq=DEC_ROWS, valid=valid, stack=(DEPTH, stacks[0]))
    o, gdn_new = gdn_decode(proj, p, st_gdn, layer, nb=bsz, valid=valid, stack=(DEPTH, stacks[1]))
    return y, o, ssm_new, gdn_new


def _conv_cols_to_ref_order(rows):
    ssm = jnp.concatenate([rows[..., P_X:P_X + D_SSM], rows[..., P_B:P_B + SSM_GN], rows[..., P_C:P_C + SSM_GN]], axis=-1)
    return ssm, rows[..., P_Q:P_Q + GDN_CONV_DIM]


def _decode_blocks(proj_s, bsz, seq, st_ssm_conv, st_gdn_conv):
    hist = jnp.zeros((bsz, CONV_K - 1, P_COLS), F32)
    hist = hist.at[..., P_X:P_X + D_SSM].set(st_ssm_conv[..., :D_SSM])
    hist = hist.at[..., P_B:P_B + SSM_GN].set(st_ssm_conv[..., D_SSM:D_SSM + SSM_GN])
    hist = hist.at[..., P_C:P_C + SSM_GN].set(st_ssm_conv[..., D_SSM + SSM_GN:])
    hist = hist.at[..., P_Q:P_Q + GDN_CONV_DIM].set(st_gdn_conv)
    lead = jnp.zeros((bsz, DEC_TOKEN_ROW - (CONV_K - 1), P_COLS), F32)
    trail = jnp.zeros((bsz, DEC_ROWS - DEC_TOKEN_ROW - seq, P_COLS), F32)
    blocks = jnp.concatenate([lead, hist, proj_s.reshape(bsz, seq, P_COLS), trail], axis=1)
    return blocks.reshape(bsz * DEC_ROWS, P_COLS)


def _reorder_w_in(w):
    small = jnp.concatenate([w[:, SPLIT_XBC:SPLIT_DT], w[:, SPLIT_GATE:]], axis=1)
    small = jnp.pad(small, ((0, 0), (0, P_COLS - P_SMALL - small.shape[1])))
    xbc = SPLIT_Z
    return jnp.concatenate([w[:, :SPLIT_Z], w[:, xbc:xbc + D_SSM], w[:, SPLIT_DT:SPLIT_GATE],
                            w[:, xbc + D_SSM:xbc + D_SSM + 2 * SSM_GN], small], axis=1).astype(BF16)


def _dense_ffn(h, x, wg, wu, wd):
    m = h.shape[0]
    tm = m // 4
    sub = tm // 2
    nt = m // tm
    te = jnp.zeros((nt,), jnp.int32)
    tmap = jnp.arange(nt, dtype=jnp.int32)
    nv = jnp.full((nt,), tm // sub, jnp.int32)
    hid = ffn_up(h, wg, wu, te, tmap, nv, tmap * (tm // LANES), tm=tm, sub=sub, gr=sub, tn=256)
    return ffn_down(hid, wd, te, tmap, nv, x, tm=tm, sub=sub, gr=sub, tn=1024, tk=1024)


MOE_TILE = 3072
MOE_SUB = 1024
MOE_GRANULE = LANES


def _moe_ffn(h, x, route, wg, wu, wd):
    m = h.shape[0]
    tm, sub, gr = MOE_TILE, MOE_SUB, MOE_GRANULE
    nt = (m * TOP_K) // tm + N_EXPERTS
    top_i = route[:, :TOP_K].astype(jnp.int32)
    probs = route[:, TOP_K:2 * TOP_K]
    e_flat = top_i.reshape(-1)
    n_assign = e_flat.shape[0]
    onehot = (e_flat[:, None] == jnp.arange(N_EXPERTS, dtype=jnp.int32)[None, :]).astype(jnp.int32)
    csum = jnp.cumsum(onehot, axis=0)
    counts = csum[-1]
    rank = jnp.sum(onehot * csum, axis=1) - 1
    padded = (counts + gr - 1) // gr * gr
    group_start = jnp.cumsum(padded) - padded
    in_pos = jnp.sum(onehot * group_start[None, :], axis=1) + rank
    n_rows = n_assign + N_EXPERTS * gr + tm
    src = jnp.zeros((n_rows,), jnp.int32).at[in_pos].set(jnp.arange(n_assign, dtype=jnp.int32) // TOP_K)
    tiles_per_e = (counts + tm - 1) // tm
    tile_end_e = jnp.cumsum(tiles_per_e)
    tile_start_e = tile_end_e - tiles_per_e
    out_pos = jnp.sum(onehot * tile_start_e[None, :], axis=1) * tm + rank
    n_used = tile_end_e[-1]
    tile_ids = jnp.arange(nt, dtype=jnp.int32)
    tmap = jnp.minimum(tile_ids, n_used - 1)
    te = jnp.minimum(jnp.sum((tile_end_e[None, :] <= tmap[:, None]).astype(jnp.int32), axis=1), N_EXPERTS - 1)
    k_in_e = tmap - tile_start_e[te]
    rows_in_tile = jnp.clip(counts[te] - k_in_e * tm, 0, tm)
    nv = jnp.where(tile_ids < n_used, (rows_in_tile + gr - 1) // gr, 0).astype(jnp.int32)
    rg = ((group_start[te] + k_in_e * tm) // LANES).astype(jnp.int32)

    xs = h.at[src].get(mode="promise_in_bounds")
    hid = ffn_up(xs, wg, wu, te, tmap, nv, rg, tm=tm, sub=sub, gr=gr, tn=256)
    ys = ffn_down(hid, wd, te, tmap, nv, tm=tm, sub=sub, gr=gr, tn=1024, tk=1024)
    out_pos = out_pos.reshape(m, TOP_K)
    out = probs[:, 0:1] * ys.at[out_pos[:, 0]].get(mode="promise_in_bounds")
    out = out + probs[:, 1:2] * ys.at[out_pos[:, 1]].get(mode="promise_in_bounds")
    return x + out


def kernel(x_prompt, x_sample, state_ssm, state_ssm_conv, state_gdn, state_gdn_conv, norm_mix, w_in, ssm_conv_w, ssm_conv_b, ssm_dt_bias, ssm_a_log, ssm_d, ssm_norm, gdn_conv_w, gdn_dt_bias, gdn_a_log, gdn_norm, w_out, norm_ffn, dense_w_gate, dense_w_up, dense_w_down, moe_router, moe_w_gate, moe_w_up, moe_w_down, norm_final):
    bp, lp, d = x_prompt.shape
    bs, ls, _ = x_sample.shape
    mp, ms = bp * lp, bs * ls
    x = jnp.concatenate([x_prompt.reshape(mp, d), x_sample.reshape(ms, d)], axis=0)
    m = mp + ms
    tm_all = m // 8

    tm_out = 1024
    zero_ssm = jnp.zeros((1, bp, D_SSM, D_STATE), F32)
    zero_gdn = jnp.zeros((1, bp, GDN_HEADS, GDN_HEAD_DIM, GDN_HEAD_DIM), F32)
    ssm_states = state_ssm.reshape(DEPTH, bs, D_SSM, D_STATE)
    dec_stacks = (None, None)
    new_p = [[], [], [], []]
    new_s = [[], [], [], []]
    for i in range(DEPTH):
        p = dict(ssm_conv_w=ssm_conv_w[i], ssm_conv_b=ssm_conv_b[i], ssm_dt_bias=ssm_dt_bias[i],
                 ssm_a_log=ssm_a_log[i], ssm_d=ssm_d[i], ssm_norm=ssm_norm[i], gdn_conv_w=gdn_conv_w[i],
                 gdn_dt_bias=gdn_dt_bias[i], gdn_a_log=gdn_a_log[i], gdn_norm=gdn_norm[i])
        h = rmsnorm(x, norm_mix[i], BF16)
        proj = matmul(h, _reorder_w_in(w_in[i]), tm=tm_all, tn=512, sub=tm_all // 2)
        y_p, o_p, ssm_p, gdn_p = _mixers(proj, bp, lp, zero_ssm, zero_gdn, 0, p)
        proj_s = proj[mp:]
        y_s, o_s, ssm_s, gdn_s = _mixers(_decode_blocks(proj_s, bs, ls, state_ssm_conv[i], state_gdn_conv[i]),
                                         bs, ls, ssm_states, state_gdn, i, p, stacks=dec_stacks)
        dec_stacks = (ssm_s, gdn_s)
        tail_p = jnp.stack([proj[b * lp + lp - (CONV_K - 1):(b + 1) * lp] for b in range(bp)])
        conv_p = _conv_cols_to_ref_order(tail_p)
        ssm_tail, gdn_tail = _conv_cols_to_ref_order(proj_s.reshape(bs, ls, P_COLS))
        conv_s = (jnp.concatenate([state_ssm_conv[i], ssm_tail], axis=1)[:, ls:],
                  jnp.concatenate([state_gdn_conv[i], gdn_tail], axis=1)[:, ls:])
        for k, (vp, vs) in enumerate(((ssm_p, None), (conv_p[0], conv_s[0]), (gdn_p, None), (conv_p[1], conv_s[1]))):
            new_p[k].append(vp)
            new_s[k].append(vs)
        tok = lambda t: jnp.pad(t.reshape(bs, DEC_ROWS, -1)[:, DEC_TOKEN_ROW:DEC_TOKEN_ROW + ls].reshape(ms, -1),
                                ((0, tm_out - ms), (0, 0)))
        x = out_proj(y_p, o_p, tok(y_s), tok(o_s), w_out[i], x, tm=tm_out, tn=512, sub=256)
        j = i // 2
        if i % 2 == 0:
            h = rmsnorm(x, norm_ffn[i], BF16)
            x = _dense_ffn(h, x, dense_w_gate[j:j + 1], dense_w_up[j:j + 1], dense_w_down[j:j + 1])
        else:
            h, route = rmsnorm_router(x, norm_ffn[i], moe_router[j])
            x = _moe_ffn(h, x, route, moe_w_gate[j], moe_w_up[j], moe_w_down[j])
    y = rmsnorm(x, norm_final, F32)
    return (y[:mp].reshape(bp, lp, d), y[mp:].reshape(bs, ls, d),
            jnp.stack(new_p[0]), jnp.stack(new_p[1]), jnp.stack(new_p[2]), jnp.stack(new_p[3]),
            dec_stacks[0].reshape(DEPTH, bs, SSM_HEADS, SSM_HEAD_DIM, D_STATE), jnp.stack(new_s[1]), dec_stacks[1], jnp.stack(new_s[3]))
```

```python
import functools

import jax
import jax.numpy as jnp
from jax import lax
from jax.experimental import pallas as pl
from jax.experimental.pallas import tpu as pltpu

F32 = jnp.float32
BF16 = jnp.bfloat16

D_MODEL = 4096
DEPTH = 2
D_SSM = 2048
D_GDN = 2048
SSM_HEAD_DIM = 64
SSM_HEADS = 32
SSM_GROUPS = 4
SSM_HPG = 8
D_STATE = 128
SSM_GN = SSM_GROUPS * D_STATE
SSM_CONV_DIM = D_SSM + 2 * SSM_GN
GDN_HEAD_DIM = 128
GDN_HEADS = 16
GDN_CONV_DIM = 3 * D_GDN
CONV_K = 4
CHUNK = 128
SPLIT_Z = D_SSM
SPLIT_XBC = SPLIT_Z + SSM_CONV_DIM
SPLIT_DT = SPLIT_XBC + SSM_HEADS
SPLIT_QKV = SPLIT_DT + GDN_CONV_DIM
SPLIT_GATE = SPLIT_QKV + D_GDN
SPLIT_BETA = SPLIT_GATE + GDN_HEADS
IN_COLS = SPLIT_BETA + GDN_HEADS
D_FF = 14336
N_EXPERTS = 8
TOP_K = 2
EPS = 1e-6

P_Z, P_X, P_Q, P_K, P_V, P_GATE = 0, 2048, 4096, 6144, 8192, 10240
P_B, P_C, P_SMALL = 12288, 12800, 13312
P_COLS = 13824
S_DT, S_BETA, S_A = 0, SSM_HEADS, SSM_HEADS + GDN_HEADS
CONV_HIST = 8

LANES = 128
T_PAD = 128
DEC_ROWS = 16
VMEM_LIMIT = 56 * 1024 * 1024


def _cparams(sem):
    return pltpu.CompilerParams(dimension_semantics=sem, vmem_limit_bytes=VMEM_LIMIT)


def _dot(a, b):
    return jnp.dot(a.astype(BF16), b.astype(BF16), preferred_element_type=F32)


def _dot_nt(a, b):
    return lax.dot_general(a.astype(BF16), b.astype(BF16), (((1,), (1,)), ((), ())),
                           preferred_element_type=F32)


def _dot_tn(a, b):
    return lax.dot_general(a.astype(BF16), b.astype(BF16), (((0,), (0,)), ((), ())),
                           preferred_element_type=F32)


def _split3(a):
    a1 = a.astype(BF16)
    r1 = a - a1.astype(F32)
    a2 = r1.astype(BF16)
    a3 = (r1 - a2.astype(F32)).astype(BF16)
    return a1, a2, a3


def _dot_sel_l(sel, a):
    a1, a2, a3 = _split3(a)
    s = sel.astype(BF16)
    d = lambda p: jnp.dot(s, p, preferred_element_type=F32)
    return (d(a3) + d(a2)) + d(a1)


def _dot_sel_r(a, sel):
    a1, a2, a3 = _split3(a)
    s = sel.astype(BF16)
    d = lambda p: jnp.dot(p, s, preferred_element_type=F32)
    return (d(a3) + d(a2)) + d(a1)


def _dot_hi(a, b):
    a1 = a.astype(BF16)
    a2 = (a - a1.astype(F32)).astype(BF16)
    b1 = b.astype(BF16)
    b2 = (b - b1.astype(F32)).astype(BF16)
    d = lambda p, q: jnp.dot(p, q, preferred_element_type=F32)
    return (d(a2, b1) + d(a1, b2)) + d(a1, b1)


def _pad_rows(v, rows):
    if v.shape[0] == rows:
        return v
    return jnp.concatenate([v, jnp.zeros((rows - v.shape[0], v.shape[1]), v.dtype)], axis=0)


def _silu(v):
    return v * jax.nn.sigmoid(v)


def _causal_conv(raw_ref, buf_ref, w_ref, tq, first_chunk):
    @pl.when(first_chunk)
    def _():
        buf_ref[0:CONV_HIST, :] = jnp.zeros((CONV_HIST, buf_ref.shape[1]), F32)

    r = raw_ref[...]
    tail = buf_ref[0:CONV_HIST, :]
    w = w_ref[...]
    first_rows = lax.broadcasted_iota(jnp.int32, (CONV_HIST, r.shape[1]), 0)
    y = None
    for j in range(CONV_K - 1):
        s = CONV_K - 1 - j
        rolled = pltpu.roll(r, s, 0)
        top = jnp.where(first_rows < s, pltpu.roll(tail, s, 0), rolled[:CONV_HIST])
        shifted = top if tq == CONV_HIST else jnp.concatenate([top, rolled[CONV_HIST:]], axis=0)
        term = w[j:j + 1] * shifted
        y = term if y is None else y + term
    y = y + w[CONV_K - 1:CONV_K] * r
    buf_ref[0:CONV_HIST, :] = r[tq - CONV_HIST:]
    return y


def _small_and_transpose(small_ref):
    raw = _pad_rows(small_ref[...], T_PAD)
    row = lax.broadcasted_iota(jnp.int32, (T_PAD, T_PAD), 0)
    col = lax.broadcasted_iota(jnp.int32, (T_PAD, T_PAD), 1)
    eye = (row == col).astype(BF16)
    a1, a2, a3 = _split3(raw)
    d = lambda p: lax.dot_general(p, eye, (((0,), (0,)), ((), ())), preferred_element_type=F32)
    return raw, (d(a3) + d(a2)) + d(a1)


def _valid_masks(valid, width):
    lo, hi = valid
    r = lax.broadcasted_iota(jnp.int32, (T_PAD, width), 0)
    c = lax.broadcasted_iota(jnp.int32, (width, T_PAD), 1)
    return (r >= lo) & (r < hi), (c >= lo) & (c < hi)


def _rmsnorm_kernel(x_ref, g_ref, o_ref):
    x = x_ref[...]
    ms = jnp.mean(x * x, axis=-1, keepdims=True)
    o_ref[...] = (x * lax.rsqrt(ms + EPS) * g_ref[...]).astype(o_ref.dtype)


def rmsnorm(x, gain, out_dtype, tm=256):
    m, d = x.shape
    return pl.pallas_call(
        _rmsnorm_kernel,
        grid=(m // tm,),
        in_specs=[pl.BlockSpec((tm, d), lambda i: (i, 0)), pl.BlockSpec((1, d), lambda i: (0, 0))],
        out_specs=pl.BlockSpec((tm, d), lambda i: (i, 0)),
        out_shape=jax.ShapeDtypeStruct((m, d), out_dtype),
        compiler_params=_cparams(("parallel",)),
        name="rmsnorm",
    )(x, gain.reshape(1, d))


def _rmsnorm_router_kernel(x_ref, g_ref, r_ref, o_ref, route_ref):
    x = x_ref[...]
    ms = jnp.mean(x * x, axis=-1, keepdims=True)
    h = x * lax.rsqrt(ms + EPS) * g_ref[...]
    o_ref[...] = h.astype(o_ref.dtype)
    logits = _dot_hi(h, r_ref[...])
    lane = lax.broadcasted_iota(jnp.int32, logits.shape, 1)
    neg = jnp.float32(-jnp.inf)
    lm = jnp.where(lane < N_EXPERTS, logits, neg)
    m1 = jnp.max(lm, axis=-1, keepdims=True)
    i1 = jnp.min(jnp.where(lm == m1, lane, LANES), axis=-1, keepdims=True)
    lm2 = jnp.where(lane == i1, neg, lm)
    m2 = jnp.max(lm2, axis=-1, keepdims=True)
    i2 = jnp.min(jnp.where(lm2 == m2, lane, LANES), axis=-1, keepdims=True)
    e = jnp.exp(m2 - m1)
    p1 = 1.0 / (1.0 + e)
    p2 = e / (1.0 + e)
    route_ref[...] = jnp.where(lane == 0, i1.astype(F32),
                               jnp.where(lane == 1, i2.astype(F32),
                                         jnp.where(lane == 2, p1, jnp.where(lane == 3, p2, 0.0))))


def rmsnorm_router(x, gain, router, tm=256):
    m, d = x.shape
    r_pad = jnp.zeros((d, LANES), F32).at[:, :N_EXPERTS].set(router)
    return pl.pallas_call(
        _rmsnorm_router_kernel,
        grid=(m // tm,),
        in_specs=[pl.BlockSpec((tm, d), lambda i: (i, 0)), pl.BlockSpec((1, d), lambda i: (0, 0)),
                  pl.BlockSpec((d, LANES), lambda i: (0, 0))],
        out_specs=[pl.BlockSpec((tm, d), lambda i: (i, 0)), pl.BlockSpec((tm, LANES), lambda i: (i, 0))],
        out_shape=[jax.ShapeDtypeStruct((m, d), BF16), jax.ShapeDtypeStruct((m, LANES), F32)],
        compiler_params=_cparams(("parallel",)),
        name="rmsnorm_router",
    )(x, gain.reshape(1, d), r_pad)


def _matmul_kernel(*refs, sub, nsub, has_res):
    if has_res:
        x_ref, w_ref, r_ref, o_ref = refs
    else:
        x_ref, w_ref, o_ref = refs
        r_ref = None

    def body(i, carry):
        rows = pl.ds(pl.multiple_of(i * sub, sub), sub)
        acc = jnp.dot(x_ref[rows, :], w_ref[...], preferred_element_type=F32)
        if has_res:
            acc = acc + r_ref[rows, :]
        o_ref[rows, :] = acc.astype(o_ref.dtype)
        return carry

    lax.fori_loop(0, nsub, body, 0)


def matmul(x, w, res=None, *, tm, tn, sub, out_dtype=F32):
    m, k = x.shape
    n = w.shape[1]
    in_specs = [pl.BlockSpec((tm, k), lambda i, j: (i, 0)), pl.BlockSpec((k, tn), lambda i, j: (0, j))]
    args = [x, w]
    if res is not None:
        in_specs.append(pl.BlockSpec((tm, tn), lambda i, j: (i, j)))
        args.append(res)
    return pl.pallas_call(
        functools.partial(_matmul_kernel, sub=sub, nsub=tm // sub, has_res=res is not None),
        grid=(m // tm, n // tn),
        in_specs=in_specs,
        out_specs=pl.BlockSpec((tm, tn), lambda i, j: (i, j)),
        out_shape=jax.ShapeDtypeStruct((m, n), out_dtype),
        compiler_params=_cparams(("parallel", "arbitrary")),
        name="matmul",
    )(*args)


def _out_proj_kernel(yp_ref, op_ref, ys_ref, os_ref, wy_ref, wo_ref, r_ref, o_ref, *, sub, nsub, n_prompt_tiles):
    def run(y_ref, g_ref):
        def body(i, carry):
            rows = pl.ds(pl.multiple_of(i * sub, sub), sub)
            acc = jnp.dot(y_ref[rows, :], wy_ref[...].astype(BF16), preferred_element_type=F32)
            acc = acc + jnp.dot(g_ref[rows, :], wo_ref[...].astype(BF16), preferred_element_type=F32)
            o_ref[rows, :] = acc + r_ref[rows, :]
            return carry

        lax.fori_loop(0, nsub, body, 0)

    is_prompt = pl.program_id(0) < n_prompt_tiles

    @pl.when(is_prompt)
    def _():
        run(yp_ref, op_ref)

    @pl.when(jnp.logical_not(is_prompt))
    def _():
        run(ys_ref, os_ref)


def out_proj(y_p, o_p, y_s, o_s, w, res, *, tm, tn, sub):
    mp, k = y_p.shape
    m, n = res.shape
    npt = mp // tm
    nt = npt + y_s.shape[0] // tm
    p_blk = pl.BlockSpec((tm, k), lambda i, j: (jnp.minimum(i, npt - 1), 0))
    s_blk = pl.BlockSpec((tm, k), lambda i, j: (jnp.maximum(i - npt, 0), 0))
    return pl.pallas_call(
        functools.partial(_out_proj_kernel, sub=sub, nsub=tm // sub, n_prompt_tiles=npt),
        grid=(nt, n // tn),
        in_specs=[p_blk, p_blk, s_blk, s_blk,
                  pl.BlockSpec((k, tn), lambda i, j: (0, j)), pl.BlockSpec((k, tn), lambda i, j: (1, j)),
                  pl.BlockSpec((tm, tn), lambda i, j: (i, j))],
        out_specs=pl.BlockSpec((tm, tn), lambda i, j: (i, j)),
        out_shape=jax.ShapeDtypeStruct((m, n), F32),
        compiler_params=_cparams(("parallel", "arbitrary")),
        name="out_proj",
    )(y_p, o_p, y_s, o_s, w, w, res)


def _row_blocks(nv, sub, gr, make):
    sizes = [sub]
    if sub // 2 > gr and (sub // 2) % gr == 0:
        sizes.append(sub // 2)
    if gr < sub:
        sizes.append(gr)
    done = 0
    for size in sizes:
        per = size // gr
        n = (nv - done) // per
        run = make(size)

        def body(i, carry, run=run, size=size, start=done):
            run(pl.multiple_of(start * gr + i * size, gr))
            return carry

        lax.fori_loop(0, n, body, 0)
        done = done + n * per


def _ffn_up_kernel(te_ref, tmap_ref, nv_ref, rg_ref, x_ref, wg_ref, wu_ref, o_ref, *, sub, gr, ngr):
    nv = nv_ref[pl.program_id(0)]

    @pl.when(nv > 0)
    def _():
        def block(size):
            def run(r0):
                rows = pl.ds(r0, size)
                xs = x_ref[rows, :]
                g = jnp.dot(xs, wg_ref[0].astype(BF16), preferred_element_type=F32)
                u = jnp.dot(xs, wu_ref[0].astype(BF16), preferred_element_type=F32)
                o_ref[rows, :] = (_silu(g) * u).astype(o_ref.dtype)
            return run

        _row_blocks(nv, sub, gr, block)

        def zero(i, carry):
            rows = pl.ds(pl.multiple_of(i * gr, gr), gr)
            o_ref[rows, :] = jnp.zeros((gr, o_ref.shape[1]), o_ref.dtype)
            return carry

        lax.fori_loop(nv, ngr, zero, 0)


def ffn_up(x, wg, wu, tile_e, tile_map, tile_nv, tile_rg, *, tm, sub, gr, tn):
    d = x.shape[1]
    nt = tile_e.shape[0]
    f = wg.shape[2]
    nj = f // tn

    def wmap(t, j, te, tmap, nv, rg):
        return (te[t], 0, jnp.where(nv[t] > 0, j, nj - 1))

    return pl.pallas_call(
        functools.partial(_ffn_up_kernel, sub=sub, gr=gr, ngr=tm // gr),
        grid_spec=pltpu.PrefetchScalarGridSpec(
            num_scalar_prefetch=4,
            grid=(nt, nj),
            in_specs=[pl.BlockSpec((pl.Element(tm), pl.Element(d)), lambda t, j, te, tmap, nv, rg: (rg[t] * LANES, 0),
                                   pipeline_mode=pl.Buffered(1)),
                      pl.BlockSpec((1, d, tn), wmap),
                      pl.BlockSpec((1, d, tn), wmap)],
            out_specs=pl.BlockSpec((tm, tn), lambda t, j, te, tmap, nv, rg: (tmap[t], jnp.where(nv[t] > 0, j, nj - 1))),
        ),
        out_shape=jax.ShapeDtypeStruct((nt * tm, f), BF16),
        compiler_params=_cparams(("arbitrary", "arbitrary")),
        name="ffn_up",
    )(tile_e, tile_map, tile_nv, tile_rg, x, wg, wu)


def _ffn_down_kernel(te_ref, tmap_ref, nv_ref, x_ref, w_ref, *rest, sub, gr, has_res):
    if has_res:
        r_ref, o_ref = rest
    else:
        (o_ref,) = rest
    k = pl.program_id(2)
    nv = nv_ref[pl.program_id(0)]

    @pl.when(nv > 0)
    def _():
        @pl.when(k == 0)
        def _():
            o_ref[...] = r_ref[...] if has_res else jnp.zeros(o_ref.shape, o_ref.dtype)

        def block(size):
            def run(r0):
                rows = pl.ds(r0, size)
                o_ref[rows, :] += jnp.dot(x_ref[rows, :], w_ref[0].astype(BF16), preferred_element_type=F32)
            return run

        _row_blocks(nv, sub, gr, block)


def ffn_down(x, wd, tile_e, tile_map, tile_nv, res=None, *, tm, sub, gr, tn, tk):
    rows, f = x.shape
    nt = rows // tm
    d = wd.shape[2]
    nj, nk = d // tn, f // tk

    def live(t, idx, last, nv):
        return jnp.where(nv[t] > 0, idx, last)

    omap = lambda t, j, k, te, tmap, nv: (tmap[t], live(t, j, nj - 1, nv))
    in_specs = [pl.BlockSpec((tm, tk), lambda t, j, k, te, tmap, nv: (tmap[t], live(t, k, nk - 1, nv))),
                pl.BlockSpec((1, tk, tn), lambda t, j, k, te, tmap, nv:
                             (te[t], live(t, k, nk - 1, nv), live(t, j, nj - 1, nv)))]
    args = [tile_e, tile_map, tile_nv, x, wd]
    if res is not None:
        in_specs.append(pl.BlockSpec((tm, tn), omap))
        args.append(res)
    return pl.pallas_call(
        functools.partial(_ffn_down_kernel, sub=sub, gr=gr, has_res=res is not None),
        grid_spec=pltpu.PrefetchScalarGridSpec(
            num_scalar_prefetch=3,
            grid=(nt, nj, nk),
            in_specs=in_specs,
            out_specs=pl.BlockSpec((tm, tn), omap),
        ),
        out_shape=jax.ShapeDtypeStruct((rows, d), F32),
        compiler_params=_cparams(("arbitrary", "arbitrary", "arbitrary")),
        name="ffn_down",
    )(*args)


def _ssd_kernel(z_ref, xraw_ref, braw_ref, craw_ref, small_ref, wx_ref, wb_ref, wc_ref, bx_ref, bb_ref, bc_ref,
                dtb_ref, dtbt_ref, alog_ref, alogt_ref, dexp_ref, norm_ref, h0_ref, *rest, tq, valid):
    T = T_PAD
    y_ref, h_ref, bufx, bufb, bufc = rest[-5:]

    @pl.when(pl.program_id(1) == 0)
    def _():
        h_ref[...] = h0_ref[...]

    row = lax.broadcasted_iota(jnp.int32, (T, T), 0)
    col = lax.broadcasted_iota(jnp.int32, (T, T), 1)
    incl = (row >= col)
    tri_l = incl.astype(F32)
    tri_u = (row <= col).astype(F32)

    first_chunk = pl.program_id(1) == 0
    x_act = _silu(_causal_conv(xraw_ref, bufx, wx_ref, tq, first_chunk) + bx_ref[...])
    b_act = _silu(_causal_conv(braw_ref, bufb, wb_ref, tq, first_chunk) + bb_ref[...])
    c_act = _silu(_causal_conv(craw_ref, bufc, wc_ref, tq, first_chunk) + bc_ref[...])
    raw, raw_t = _small_and_transpose(small_ref)
    rmask, cmask = _valid_masks(valid, SSM_HEADS)
    dt = jnp.where(rmask, jax.nn.softplus(raw[:, S_DT:S_DT + SSM_HEADS] + dtb_ref[...]), 0.0)
    dtt = jnp.where(cmask, jax.nn.softplus(raw_t[S_DT:S_DT + SSM_HEADS, :] + dtbt_ref[...]), 0.0)
    a = dt * (-jnp.exp(alog_ref[...]))
    at = dtt * (-jnp.exp(alogt_ref[...]))
    acs = _dot_sel_l(tri_l, a)
    acst = _dot_sel_r(at, tri_u)
    last = acs[T - 1:T, :]
    e_last = jnp.exp(last)
    wts = jnp.exp(last - acs) * dt

    hp = lax.broadcasted_iota(jnp.int32, (SSM_HEADS, D_SSM), 1) // SSM_HEAD_DIM
    expand = (hp == lax.broadcasted_iota(jnp.int32, (SSM_HEADS, D_SSM), 0)).astype(F32)
    expand_b = expand.astype(BF16)
    lane_expand = lambda v: jnp.dot(v.astype(BF16), expand_b, preferred_element_type=F32)
    eacs_e = lane_expand(jnp.exp(acs[:tq]))
    xdt = _pad_rows((x_act * lane_expand(dt[:tq])).astype(BF16), T)
    xw = _pad_rows((x_act * lane_expand(wts[:tq])).astype(BF16), T)
    incl_q = incl[:tq]
    half = lax.broadcasted_iota(jnp.int32, (tq, LANES), 1) < SSM_HEAD_DIM
    gw = SSM_HPG * SSM_HEAD_DIM

    y_groups = []
    for g in range(SSM_GROUPS):
        bg = _pad_rows(b_act[:, g * D_STATE:(g + 1) * D_STATE], T).astype(BF16)
        cg = c_act[:, g * D_STATE:(g + 1) * D_STATE].astype(BF16)
        cb = _dot_nt(cg, bg)
        hg = h_ref[0, g * gw:(g + 1) * gw, :]
        y_state = _dot_nt(cg, hg)
        pieces = []
        for j in range(SSM_HPG // 2):
            ys = []
            for hh in (2 * j, 2 * j + 1):
                h = g * SSM_HPG + hh
                seg = acs[:tq, h:h + 1] - acst[h:h + 1, :]
                w = cb * jnp.exp(jnp.where(incl_q, seg, -jnp.inf))
                lo = g * gw + j * LANES
                ys.append(jnp.dot(w.astype(BF16), xdt[:, lo:lo + LANES], preferred_element_type=F32))
            pieces.append(jnp.where(half, ys[0], ys[1]))
        y_intra = jnp.concatenate(pieces, axis=1)
        y_groups.append(y_intra + eacs_e[:, g * gw:(g + 1) * gw] * y_state)
        upd = _dot_tn(xw[:, g * gw:(g + 1) * gw], bg)
        for hh in range(SSM_HPG):
            h = g * SSM_HPG + hh
            r0 = g * gw + hh * SSM_HEAD_DIM
            h_ref[0, r0:r0 + SSM_HEAD_DIM, :] = (e_last[:, h:h + 1] * hg[hh * SSM_HEAD_DIM:(hh + 1) * SSM_HEAD_DIM, :]
                                                 + upd[hh * SSM_HEAD_DIM:(hh + 1) * SSM_HEAD_DIM, :])

    y = jnp.concatenate(y_groups, axis=1)
    y = y + dexp_ref[...] * x_act
    y = y * _silu(z_ref[...])
    outs = []
    for g in range(SSM_GROUPS):
        yg = y[:, g * gw:(g + 1) * gw]
        ms = jnp.mean(yg * yg, axis=-1, keepdims=True)
        outs.append(yg * lax.rsqrt(ms + EPS))
    y_ref[...] = (jnp.concatenate(outs, axis=1) * norm_ref[...]).astype(y_ref.dtype)


def _proj_block(tq, nc, width, offset):
    return pl.BlockSpec((tq, width), lambda b, c: (b * nc + c, offset // width))


def _full_block(shape):
    return pl.BlockSpec(shape, lambda b, c: tuple(0 for _ in shape))


def _stacked_state(shape, nb, layer, stack):
    zeros = (0,) * len(shape)
    if stack is None:
        return pl.BlockSpec((1,) + shape, lambda b, c: (b,) + zeros), (nb,) + shape, [], [], {}
    depth, prev = stack
    spec = pl.BlockSpec((None, 1) + shape, lambda b, c: (layer, b) + zeros)
    if prev is None:
        return spec, (depth, nb) + shape, [], [], {}
    return spec, (depth, nb) + shape, [pl.BlockSpec(memory_space=pl.ANY)], [prev], None


def ssd_scan(proj, p, h0, layer, *, nb, nc, tq, valid, stack=None):
    rows = nb * nc * tq
    cw, cb = p['ssm_conv_w'], p['ssm_conv_b'].reshape(1, SSM_CONV_DIM)
    xs, bs, cs = slice(0, D_SSM), slice(D_SSM, D_SSM + SSM_GN), slice(D_SSM + SSM_GN, SSM_CONV_DIM)
    state_in = pl.BlockSpec((None, 1, D_SSM, D_STATE), lambda b, c: (layer, b, 0, 0))
    state, state_shape, extra_specs, extra_args, aliases = _stacked_state((D_SSM, D_STATE), nb, layer, stack)
    in_specs = [_proj_block(tq, nc, D_SSM, P_Z), _proj_block(tq, nc, D_SSM, P_X), _proj_block(tq, nc, SSM_GN, P_B),
                _proj_block(tq, nc, SSM_GN, P_C), _proj_block(tq, nc, LANES, P_SMALL),
                _full_block((CONV_K, D_SSM)), _full_block((CONV_K, SSM_GN)), _full_block((CONV_K, SSM_GN)),
                _full_block((1, D_SSM)), _full_block((1, SSM_GN)), _full_block((1, SSM_GN)),
                _full_block((1, SSM_HEADS)), _full_block((SSM_HEADS, 1)),
                _full_block((1, SSM_HEADS)), _full_block((SSM_HEADS, 1)), _full_block((1, D_SSM)), _full_block((1, D_SSM)),
                state_in] + extra_specs
    if aliases is None:
        aliases = {len(in_specs) - 1: 1}
    return pl.pallas_call(
        functools.partial(_ssd_kernel, tq=tq, valid=valid),
        grid=(nb, nc),
        in_specs=in_specs,
        out_specs=[pl.BlockSpec((tq, D_SSM), lambda b, c: (b * nc + c, 0)), state],
        out_shape=[jax.ShapeDtypeStruct((rows, D_SSM), BF16), jax.ShapeDtypeStruct(state_shape, F32)],
        scratch_shapes=[pltpu.VMEM((tq + CONV_HIST, D_SSM), F32), pltpu.VMEM((tq + CONV_HIST, SSM_GN), F32),
                        pltpu.VMEM((tq + CONV_HIST, SSM_GN), F32)],
        input_output_aliases=aliases,
        compiler_params=_cparams(("arbitrary", "arbitrary")),
        name="ssd_scan",
    )(proj, proj, proj, proj, proj, cw[:, xs], cw[:, bs], cw[:, cs], cb[:, xs], cb[:, bs], cb[:, cs],
      p['ssm_dt_bias'].reshape(1, SSM_HEADS), p['ssm_dt_bias'].reshape(SSM_HEADS, 1),
      p['ssm_a_log'].reshape(1, SSM_HEADS), p['ssm_a_log'].reshape(SSM_HEADS, 1),
      jnp.repeat(p['ssm_d'], SSM_HEAD_DIM).reshape(1, D_SSM), p['ssm_norm'].reshape(1, D_SSM), h0, *extra_args)


GDN_PAIRS = GDN_HEADS // 2
GDN_PAIR_BATCH = 4
SOLVE_BLOCK = 8


def _bd(w):
    w = w.astype(BF16)
    half = w.shape[1] // 2
    z = jnp.zeros((w.shape[0], half), BF16)
    return jnp.concatenate([jnp.concatenate([w[:, :half], z], axis=1),
                            jnp.concatenate([z, w[:, half:]], axis=1)], axis=0)


def _wdot(x, y_wide):
    return jnp.dot(x.astype(BF16), _bd(y_wide), preferred_element_type=F32)


def _wdot_nt(x, y_wide):
    return lax.dot_general(x.astype(BF16), _bd(y_wide), (((1,), (1,)), ((), ())), preferred_element_type=F32)


def _blocked_unit_lower_inverse(a_list, eye_w, same_block):
    T = eye_w.shape[0]
    n = len(a_list)
    m0 = same_block(SOLVE_BLOCK)
    a_d = [jnp.where(m0, a, 0.0) for a in a_list]
    p = [eye_w - a for a in a_d]
    n_sq = _n_squarings(SOLVE_BLOCK)
    if n_sq > 0:
        pw = [_wdot(a, a) for a in a_d]
        for i in range(n_sq):
            if i + 1 < n_sq:
                r = [_wdot(jnp.concatenate([p[j], pw[j]], axis=0), pw[j]) for j in range(n)]
                p = [p[j] + r[j][:T] for j in range(n)]
                pw = [r[j][T:] for j in range(n)]
            else:
                p = [p[j] + _wdot(p[j], pw[j]) for j in range(n)]
    b = SOLVE_BLOCK
    while b < T:
        m_off = same_block(2 * b) & jnp.logical_not(same_block(b))
        t = [_wdot(p[j], jnp.where(m_off, a_list[j], 0.0)) for j in range(n)]
        p = [p[j] - _wdot(t[j], p[j]) for j in range(n)]
        b *= 2
    return p


def _n_squarings(n_valid):
    n_sq = 0
    while (1 << (n_sq + 1)) < n_valid:
        n_sq += 1
    return n_sq


def _gdn_kernel(qraw_ref, kraw_ref, vraw_ref, gate_ref, small_ref, wq_ref, wk_ref, wv_ref, alog_ref, alogt_ref,
                dtb_ref, dtbt_ref, norm_ref, s0_ref, o_ref, s_ref, bufq, bufk, bufv, q_s, k_s, v_s):
    T = T_PAD
    K = GDN_HEAD_DIM
    W = 2 * K

    @pl.when(pl.program_id(1) == 0)
    def _():
        s_ref[...] = s0_ref[...]

    row = lax.broadcasted_iota(jnp.int32, (T, T), 0)
    col = lax.broadcasted_iota(jnp.int32, (T, T), 1)
    tri_l = (row >= col).astype(F32)
    tri_u = (row <= col).astype(F32)
    row_w = lax.broadcasted_iota(jnp.int32, (T, W), 0)
    lane_w = lax.broadcasted_iota(jnp.int32, (T, W), 1)
    col_w = lane_w % K
    first = lane_w < K
    incl_w = row_w >= col_w
    strict_w = row_w > col_w
    eye_w = (row_w == col_w).astype(F32)
    same_block = lambda b: (row_w // b) == (col_w // b)

    first_chunk = pl.program_id(1) == 0
    q_s[...] = _silu(_causal_conv(qraw_ref, bufq, wq_ref, T, first_chunk))
    k_s[...] = _silu(_causal_conv(kraw_ref, bufk, wk_ref, T, first_chunk))
    v_s[...] = _silu(_causal_conv(vraw_ref, bufv, wv_ref, T, first_chunk))
    q_ref, k_ref, v_ref = q_s, k_s, v_s
    raw, raw_t = _small_and_transpose(small_ref)
    beta = jax.nn.sigmoid(raw[:, S_BETA:S_BETA + GDN_HEADS])
    g = -jnp.exp(alog_ref[...]) * jax.nn.softplus(raw[:, S_A:S_A + GDN_HEADS] + dtb_ref[...])
    gt = -jnp.exp(alogt_ref[...]) * jax.nn.softplus(raw_t[S_A:S_A + GDN_HEADS, :] + dtbt_ref[...])
    gcs = _dot_sel_l(tri_l, g)
    gcst = _dot_sel_r(gt, tri_u)
    egcs = jnp.exp(gcs)
    last = gcs[T - 1:T, :]
    kdec = jnp.exp(last - gcs)
    cd = jnp.exp(last)
    norm_w = jnp.concatenate([norm_ref[...], norm_ref[...]], axis=1)

    def widen(x, p):
        return jnp.where(first[:x.shape[0]], x[:, 2 * p:2 * p + 1], x[:, 2 * p + 1:2 * p + 2])

    def l2n(x):
        sq = x * x
        sa = jnp.sum(sq[:, :K], axis=-1, keepdims=True)
        sb = jnp.sum(sq[:, K:], axis=-1, keepdims=True)
        return x * lax.rsqrt(jnp.where(first, sa, sb) + EPS)

    for p0 in range(0, GDN_PAIRS, GDN_PAIR_BATCH):
        pairs = list(range(p0, p0 + GDN_PAIR_BATCH))
        qs, ks, decays, a_mats = [], [], [], []
        for p in pairs:
            lanes = slice(p * W, (p + 1) * W)
            qs.append(l2n(q_ref[:, lanes]) * (K ** -0.5))
            kw = l2n(k_ref[:, lanes])
            ks.append(kw)
            gcst_w = jnp.concatenate([gcst[2 * p:2 * p + 1, :], gcst[2 * p + 1:2 * p + 2, :]], axis=1)
            decay = jnp.exp(jnp.where(incl_w, widen(gcs, p) - gcst_w, -jnp.inf))
            decays.append(decay)
            a_mats.append(jnp.where(strict_w, widen(beta, p) * _wdot_nt(kw, kw) * decay, 0.0))
        minv = _blocked_unit_lower_inverse(a_mats, eye_w, same_block)
        us = [_wdot(minv[j], v_ref[:, p * W:(p + 1) * W] * widen(beta, p)) for j, p in enumerate(pairs)]
        ws = [_wdot(minv[j], ks[j] * (widen(beta, p) * widen(egcs, p))) for j, p in enumerate(pairs)]
        states = [jnp.concatenate([s_ref[0, 2 * p], s_ref[0, 2 * p + 1]], axis=1) for p in pairs]
        ws_qs = [_wdot(jnp.concatenate([ws[j], qs[j] * widen(egcs, p)], axis=0), states[j])
                 for j, p in enumerate(pairs)]
        v_new = [us[j] - ws_qs[j][:T] for j in range(len(pairs))]
        qk = [_wdot_nt(qs[j], ks[j]) * decays[j] for j in range(len(pairs))]
        for j, p in enumerate(pairs):
            o = ws_qs[j][T:] + _wdot(qk[j], v_new[j])
            kd = ks[j] * widen(kdec, p)
            kd_rows = jnp.concatenate([kd[:, :K], kd[:, K:]], axis=0).astype(BF16)
            upd = lax.dot_general(kd_rows, _bd(v_new[j]), (((0,), (0,)), ((), ())), preferred_element_type=F32)
            s_new = widen(cd, p)[:1] * states[j] + upd
            s_ref[0, 2 * p] = s_new[:, :K]
            s_ref[0, 2 * p + 1] = s_new[:, K:]
            sq = o * o
            ms = jnp.where(first, jnp.mean(sq[:, :K], axis=-1, keepdims=True), jnp.mean(sq[:, K:], axis=-1, keepdims=True))
            lanes = slice(p * W, (p + 1) * W)
            o = o * lax.rsqrt(ms + EPS) * norm_w * _silu(gate_ref[:, lanes])
            o_ref[:, lanes] = o.astype(o_ref.dtype)


def _gdn_params(p):
    cw = p['gdn_conv_w']
    return (cw[:, :D_GDN], cw[:, D_GDN:2 * D_GDN], cw[:, 2 * D_GDN:],
            p['gdn_a_log'].reshape(1, GDN_HEADS), p['gdn_a_log'].reshape(GDN_HEADS, 1),
            p['gdn_dt_bias'].reshape(1, GDN_HEADS), p['gdn_dt_bias'].reshape(GDN_HEADS, 1),
            p['gdn_norm'].reshape(1, GDN_HEAD_DIM))


def _gdn_specs(tq, nc, layer):
    return [_proj_block(tq, nc, D_GDN, P_Q), _proj_block(tq, nc, D_GDN, P_K), _proj_block(tq, nc, D_GDN, P_V),
            _proj_block(tq, nc, D_GDN, P_GATE), _proj_block(tq, nc, LANES, P_SMALL),
            _full_block((CONV_K, D_GDN)), _full_block((CONV_K, D_GDN)), _full_block((CONV_K, D_GDN)),
            _full_block((1, GDN_HEADS)), _full_block((GDN_HEADS, 1)), _full_block((1, GDN_HEADS)), _full_block((GDN_HEADS, 1)),
            _full_block((1, GDN_HEAD_DIM)),
            pl.BlockSpec((None, 1, GDN_HEADS, GDN_HEAD_DIM, GDN_HEAD_DIM), lambda b, c: (layer, b, 0, 0, 0))]


def gdn_scan(proj, p, s0, layer, *, nb, nc):
    tq = T_PAD
    rows = nb * nc * tq
    state = pl.BlockSpec((1, GDN_HEADS, GDN_HEAD_DIM, GDN_HEAD_DIM), lambda b, c: (b, 0, 0, 0))
    conv_buf = pltpu.VMEM((tq + CONV_HIST, D_GDN), F32)
    act = pltpu.VMEM((tq, D_GDN), F32)
    return pl.pallas_call(
        _gdn_kernel,
        grid=(nb, nc),
        in_specs=_gdn_specs(tq, nc, layer),
        out_specs=[pl.BlockSpec((tq, D_GDN), lambda b, c: (b * nc + c, 0)), state],
        out_shape=[jax.ShapeDtypeStruct((rows, D_GDN), BF16),
                   jax.ShapeDtypeStruct((nb, GDN_HEADS, GDN_HEAD_DIM, GDN_HEAD_DIM), F32)],
        scratch_shapes=[conv_buf, conv_buf, conv_buf, act, act, act],
        compiler_params=_cparams(("arbitrary", "arbitrary")),
        name="gdn_scan",
    )(proj, proj, proj, proj, proj, *_gdn_params(p), s0)


def _gdn_decode_kernel(qraw_ref, kraw_ref, vraw_ref, gate_ref, small_ref, wq_ref, wk_ref, wv_ref, alog_ref, alogt_ref,
                       dtb_ref, dtbt_ref, norm_ref, s0_ref, *rest, valid):
    R = DEC_ROWS
    K = GDN_HEAD_DIM
    N = GDN_HEADS * R
    o_ref, s_ref, bufq, bufk, bufv, q_s, k_s, v_s = rest[-8:]

    def stack(ref):
        return jnp.concatenate([ref[:, h * K:(h + 1) * K] for h in range(GDN_HEADS)], axis=0)

    row = lax.broadcasted_iota(jnp.int32, (N, N), 0)
    col = lax.broadcasted_iota(jnp.int32, (N, N), 1)
    same = (row // R) == (col // R)
    incl = same & (row >= col)
    strict = same & (row > col)
    eye = row == col
    eye_f = eye.astype(F32)

    def to_col(rowvec):
        return jnp.sum(jnp.where(eye, rowvec, 0.0), axis=1, keepdims=True)

    first_chunk = pl.program_id(1) == 0
    q_s[...] = _silu(_causal_conv(qraw_ref, bufq, wq_ref, R, first_chunk))
    k_s[...] = _silu(_causal_conv(kraw_ref, bufk, wk_ref, R, first_chunk))
    v_s[...] = _silu(_causal_conv(vraw_ref, bufv, wv_ref, R, first_chunk))
    q_ref, k_ref, v_ref = q_s, k_s, v_s
    _, raw_t = _small_and_transpose(small_ref)
    _, cmask = _valid_masks(valid, GDN_HEADS)
    beta_t = jnp.where(cmask, jax.nn.sigmoid(raw_t[S_BETA:S_BETA + GDN_HEADS, :]), 0.0)
    g_t = jnp.where(cmask, -jnp.exp(alogt_ref[...]) * jax.nn.softplus(raw_t[S_A:S_A + GDN_HEADS, :] + dtbt_ref[...]), 0.0)
    t_row = lax.broadcasted_iota(jnp.int32, (T_PAD, N), 0)
    t_col = lax.broadcasted_iota(jnp.int32, (T_PAD, N), 1)
    tile = (t_row == t_col % R).astype(F32)
    own = lax.broadcasted_iota(jnp.int32, (GDN_HEADS, N), 0) == lax.broadcasted_iota(jnp.int32, (GDN_HEADS, N), 1) // R
    flat = lambda m: jnp.sum(jnp.where(own, _dot_sel_r(m, tile), 0.0), axis=0, keepdims=True)
    g_row = flat(g_t)
    gb = jnp.concatenate([g_row, jnp.zeros((7, N), F32)], axis=0)
    csum = _dot_sel_r(gb, (same & (row <= col)).astype(F32))
    tot = _dot_sel_r(gb, same.astype(F32))
    gcs_row = csum[0:1, :]
    last_row = tot[0:1, :]
    gcs_col = to_col(gcs_row)
    bcol = to_col(flat(beta_t))
    egcs_col = jnp.exp(gcs_col)
    kdec_col = jnp.exp(to_col(last_row) - gcs_col)
    cd_row = jnp.exp(last_row)

    q_all = stack(q_ref)
    k_all = stack(k_ref)
    v_all = stack(v_ref)
    q_all = q_all * lax.rsqrt(jnp.sum(q_all * q_all, axis=-1, keepdims=True) + EPS) * (K ** -0.5)
    k_all = k_all * lax.rsqrt(jnp.sum(k_all * k_all, axis=-1, keepdims=True) + EPS)

    decay = jnp.exp(jnp.where(incl, gcs_col - gcs_row, -jnp.inf))
    a_mat = jnp.where(strict, bcol * _dot_nt(k_all, k_all) * decay, 0.0)
    minv = eye_f - a_mat
    n_sq = _n_squarings(valid[1] - valid[0])
    pw = a_mat
    for _ in range(n_sq):
        pw = _dot(pw, pw)
        minv = minv + _dot(minv, pw)
    u_all = _dot(minv, v_all * bcol)
    w_all = _dot(minv, k_all * (bcol * egcs_col))
    qd_all = q_all * egcs_col
    qk = _dot_nt(q_all, k_all) * decay
    kd_t = _dot_nt(eye_f[:K, :K], k_all * kdec_col).astype(BF16)

    first = lax.broadcasted_iota(jnp.int32, (K, 2 * K), 1) < K
    states, v_new, q_s = [], [], []
    for p in range(GDN_PAIRS):
        ra = slice(2 * p * R, (2 * p + 1) * R)
        rb = slice((2 * p + 1) * R, (2 * p + 2) * R)
        s_w = jnp.concatenate([s0_ref[0, 2 * p], s0_ref[0, 2 * p + 1]], axis=1)
        states.append(s_w)
        lhs = jnp.concatenate([jnp.concatenate([w_all[ra], w_all[rb]], axis=1),
                               jnp.concatenate([qd_all[ra], qd_all[rb]], axis=1)], axis=0)
        both = _wdot(lhs, s_w)
        v_new += [u_all[ra] - both[:R, :K], u_all[rb] - both[:R, K:]]
        q_s += [both[R:, :K], both[R:, K:]]
    v_new_all = jnp.concatenate(v_new, axis=0)
    o = jnp.concatenate(q_s, axis=0) + _dot(qk, v_new_all)
    ms = jnp.mean(o * o, axis=-1, keepdims=True)
    o = o * lax.rsqrt(ms + EPS) * norm_ref[...] * _silu(stack(gate_ref))
    for h in range(GDN_HEADS):
        o_ref[:, h * K:(h + 1) * K] = o[h * R:(h + 1) * R].astype(o_ref.dtype)
    zero = jnp.zeros((R, 2 * K), BF16)
    zhalf = jnp.zeros((R, K), BF16)
    for p in range(GDN_PAIRS):
        va = jnp.concatenate([v_new[2 * p].astype(BF16), zhalf], axis=1)
        vb = jnp.concatenate([zhalf, v_new[2 * p + 1].astype(BF16)], axis=1)
        rhs = jnp.concatenate([zero] * (2 * p) + [va, vb] + [zero] * (GDN_HEADS - 2 - 2 * p), axis=0)
        upd = jnp.dot(kd_t, rhs, preferred_element_type=F32)
        cd_w = jnp.where(first, cd_row[:, 2 * p * R:2 * p * R + 1], cd_row[:, (2 * p + 1) * R:(2 * p + 1) * R + 1])
        s_new = cd_w * states[p] + upd
        s_ref[0, 2 * p] = s_new[:, :K]
        s_ref[0, 2 * p + 1] = s_new[:, K:]


def gdn_decode(proj, p, s0, layer, *, nb, valid, stack=None):
    tq = DEC_ROWS
    rows = nb * tq
    state, state_shape, extra_specs, extra_args, aliases = _stacked_state(
        (GDN_HEADS, GDN_HEAD_DIM, GDN_HEAD_DIM), nb, layer, stack)
    in_specs = _gdn_specs(tq, 1, layer) + extra_specs
    if aliases is None:
        aliases = {len(in_specs) - 1: 1}
    conv_buf = pltpu.VMEM((tq + CONV_HIST, D_GDN), F32)
    act = pltpu.VMEM((tq, D_GDN), F32)
    return pl.pallas_call(
        functools.partial(_gdn_decode_kernel, valid=valid),
        grid=(nb, 1),
        in_specs=in_specs,
        out_specs=[pl.BlockSpec((tq, D_GDN), lambda b, c: (b, 0)), state],
        out_shape=[jax.ShapeDtypeStruct((rows, D_GDN), BF16), jax.ShapeDtypeStruct(state_shape, F32)],
        scratch_shapes=[conv_buf, conv_buf, conv_buf, act, act, act],
        input_output_aliases=aliases,
        compiler_params=_cparams(("arbitrary", "arbitrary")),
        name="gdn_decode",
    )(proj, proj, proj, proj, proj, *_gdn_params(p), s0, *extra_args)


DEC_TOKEN_ROW = 8


def _mixers(proj, bsz, seq, st_ssm, st_gdn, layer, p, stacks=None):
    if seq >= CHUNK:
        y, ssm_new = ssd_scan(proj, p, st_ssm, layer, nb=bsz, nc=seq // CHUNK, tq=T_PAD, valid=(0, T_PAD))
        o, gdn_new = gdn_scan(proj, p, st_gdn, layer, nb=bsz, nc=seq // CHUNK)
        return y, o, ssm_new.reshape(bsz, SSM_HEADS, SSM_HEAD_DIM, D_STATE), gdn_new
    valid = (DEC_TOKEN_ROW, DEC_TOKEN_ROW + seq)
    y, ssm_new = ssd_scan(proj, p, st_ssm, layer, nb=bsz, nc=1, tq=DEC_ROWS, valid=valid, stack=(DEPTH, stacks[0]))
    o, gdn_new = gdn_decode(proj, p, st_gdn, layer, nb=bsz, valid=valid, stack=(DEPTH, stacks[1]))
    return y, o, ssm_new, gdn_new


def _conv_cols_to_ref_order(rows):
    ssm = jnp.concatenate([rows[..., P_X:P_X + D_SSM], rows[..., P_B:P_B + SSM_GN], rows[..., P_C:P_C + SSM_GN]], axis=-1)
    return ssm, rows[..., P_Q:P_Q + GDN_CONV_DIM]


def _decode_blocks(proj_s, bsz, seq, st_ssm_conv, st_gdn_conv):
    hist = jnp.zeros((bsz, CONV_K - 1, P_COLS), F32)
    hist = hist.at[..., P_X:P_X + D_SSM].set(st_ssm_conv[..., :D_SSM])
    hist = hist.at[..., P_B:P_B + SSM_GN].set(st_ssm_conv[..., D_SSM:D_SSM + SSM_GN])
    hist = hist.at[..., P_C:P_C + SSM_GN].set(st_ssm_conv[..., D_SSM + SSM_GN:])
    hist = hist.at[..., P_Q:P_Q + GDN_CONV_DIM].set(st_gdn_conv)
    lead = jnp.zeros((bsz, DEC_TOKEN_ROW - (CONV_K - 1), P_COLS), F32)
    trail = jnp.zeros((bsz, DEC_ROWS - DEC_TOKEN_ROW - seq, P_COLS), F32)
    blocks = jnp.concatenate([lead, hist, proj_s.reshape(bsz, seq, P_COLS), trail], axis=1)
    return blocks.reshape(bsz * DEC_ROWS, P_COLS)


def _reorder_w_in(w):
    small = jnp.concatenate([w[:, SPLIT_XBC:SPLIT_DT], w[:, SPLIT_GATE:]], axis=1)
    small = jnp.pad(small, ((0, 0), (0, P_COLS - P_SMALL - small.shape[1])))
    xbc = SPLIT_Z
    return jnp.concatenate([w[:, :SPLIT_Z], w[:, xbc:xbc + D_SSM], w[:, SPLIT_DT:SPLIT_GATE],
                            w[:, xbc + D_SSM:xbc + D_SSM + 2 * SSM_GN], small], axis=1).astype(BF16)


def _dense_ffn(h, x, wg, wu, wd):
    m = h.shape[0]
    tm = m // 4
    sub = tm // 2
    nt = m // tm
    te = jnp.zeros((nt,), jnp.int32)
    tmap = jnp.arange(nt, dtype=jnp.int32)
    nv = jnp.full((nt,), tm // sub, jnp.int32)
    hid = ffn_up(h, wg, wu, te, tmap, nv, tmap * (tm // LANES), tm=tm, sub=sub, gr=sub, tn=256)
    return ffn_down(hid, wd, te, tmap, nv, x, tm=tm, sub=sub, gr=sub, tn=1024, tk=1024)


MOE_TILE = 3072
MOE_SUB = 1024
MOE_GRANULE = LANES


def _moe_ffn(h, x, route, wg, wu, wd):
    m = h.shape[0]
    tm, sub, gr = MOE_TILE, MOE_SUB, MOE_GRANULE
    nt = (m * TOP_K) // tm + N_EXPERTS
    top_i = route[:, :TOP_K].astype(jnp.int32)
    probs = route[:, TOP_K:2 * TOP_K]
    e_flat = top_i.reshape(-1)
    n_assign = e_flat.shape[0]
    onehot = (e_flat[:, None] == jnp.arange(N_EXPERTS, dtype=jnp.int32)[None, :]).astype(jnp.int32)
    csum = jnp.cumsum(onehot, axis=0)
    counts = csum[-1]
    rank = jnp.sum(onehot * csum, axis=1) - 1
    padded = (counts + gr - 1) // gr * gr
    group_start = jnp.cumsum(padded) - padded
    in_pos = jnp.sum(onehot * group_start[None, :], axis=1) + rank
    n_rows = n_assign + N_EXPERTS * gr + tm
    src = jnp.zeros((n_rows,), jnp.int32).at[in_pos].set(jnp.arange(n_assign, dtype=jnp.int32) // TOP_K)
    tiles_per_e = (counts + tm - 1) // tm
    tile_end_e = jnp.cumsum(tiles_per_e)
    tile_start_e = tile_end_e - tiles_per_e
    out_pos = jnp.sum(onehot * tile_start_e[None, :], axis=1) * tm + rank
    n_used = tile_end_e[-1]
    tile_ids = jnp.arange(nt, dtype=jnp.int32)
    tmap = jnp.minimum(tile_ids, n_used - 1)
    te = jnp.minimum(jnp.sum((tile_end_e[None, :] <= tmap[:, None]).astype(jnp.int32), axis=1), N_EXPERTS - 1)
    k_in_e = tmap - tile_start_e[te]
    rows_in_tile = jnp.clip(counts[te] - k_in_e * tm, 0, tm)
    nv = jnp.where(tile_ids < n_used, (rows_in_tile + gr - 1) // gr, 0).astype(jnp.int32)
    rg = ((group_start[te] + k_in_e * tm) // LANES).astype(jnp.int32)

    xs = h.at[src].get(mode="promise_in_bounds")
    hid = ffn_up(xs, wg, wu, te, tmap, nv, rg, tm=tm, sub=sub, gr=gr, tn=256)
    ys = ffn_down(hid, wd, te, tmap, nv, tm=tm, sub=sub, gr=gr, tn=1024, tk=1024)
    out_pos = out_pos.reshape(m, TOP_K)
    out = probs[:, 0:1] * ys.at[out_pos[:, 0]].get(mode="promise_in_bounds")
    out = out + probs[:, 1:2] * ys.at[out_pos[:, 1]].get(mode="promise_in_bounds")
    return x + out


def kernel(x_prompt, x_sample, state_ssm, state_ssm_conv, state_gdn, state_gdn_conv, norm_mix, w_in, ssm_conv_w, ssm_conv_b, ssm_dt_bias, ssm_a_log, ssm_d, ssm_norm, gdn_conv_w, gdn_dt_bias, gdn_a_log, gdn_norm, w_out, norm_ffn, dense_w_gate, dense_w_up, dense_w_down, moe_router, moe_w_gate, moe_w_up, moe_w_down, norm_final):
    bp, lp, d = x_prompt.shape
    bs, ls, _ = x_sample.shape
    mp, ms = bp * lp, bs * ls
    x = jnp.concatenate([x_prompt.reshape(mp, d), x_sample.reshape(ms, d)], axis=0)
    m = mp + ms
    tm_all = m // 8

    tm_out = 1024
    zero_ssm = jnp.zeros((1, bp, D_SSM, D_STATE), F32)
    zero_gdn = jnp.zeros((1, bp, GDN_HEADS, GDN_HEAD_DIM, GDN_HEAD_DIM), F32)
    ssm_states = state_ssm.reshape(DEPTH, bs, D_SSM, D_STATE)
    dec_stacks = (None, None)
    new_p = [[], [], [], []]
    new_s = [[], [], [], []]
    for i in range(DEPTH):
        p = dict(ssm_conv_w=ssm_conv_w[i], ssm_conv_b=ssm_conv_b[i], ssm_dt_bias=ssm_dt_bias[i],
                 ssm_a_log=ssm_a_log[i], ssm_d=ssm_d[i], ssm_norm=ssm_norm[i], gdn_conv_w=gdn_conv_w[i],
                 gdn_dt_bias=gdn_dt_bias[i], gdn_a_log=gdn_a_log[i], gdn_norm=gdn_norm[i])
        h = rmsnorm(x, norm_mix[i], BF16)
        proj = matmul(h, _reorder_w_in(w_in[i]), tm=tm_all, tn=512, sub=tm_all // 2)
        y_p, o_p, ssm_p, gdn_p = _mixers(proj, bp, lp, zero_ssm, zero_gdn, 0, p)
        proj_s = proj[mp:]
        y_s, o_s, ssm_s, gdn_s = _mixers(_decode_blocks(proj_s, bs, ls, state_ssm_conv[i], state_gdn_conv[i]),
                                         bs, ls, ssm_states, state_gdn, i, p, stacks=dec_stacks)
        dec_stacks = (ssm_s, gdn_s)
        tail_p = jnp.stack([proj[b * lp + lp - (CONV_K - 1):(b + 1) * lp] for b in range(bp)])
        conv_p = _conv_cols_to_ref_order(tail_p)
        ssm_tail, gdn_tail = _conv_cols_to_ref_order(proj_s.reshape(bs, ls, P_COLS))
        conv_s = (jnp.concatenate([state_ssm_conv[i], ssm_tail], axis=1)[:, ls:],
                  jnp.concatenate([state_gdn_conv[i], gdn_tail], axis=1)[:, ls:])
        for k, (vp, vs) in enumerate(((ssm_p, None), (conv_p[0], conv_s[0]), (gdn_p, None), (conv_p[1], conv_s[1]))):
            new_p[k].append(vp)
            new_s[k].append(vs)
        tok = lambda t: jnp.pad(t.reshape(bs, DEC_ROWS, -1)[:, DEC_TOKEN_ROW:DEC_TOKEN_ROW + ls].reshape(ms, -1),
                                ((0, tm_out - ms), (0, 0)))
        x = out_proj(y_p, o_p, tok(y_s), tok(o_s), w_out[i], x, tm=tm_out, tn=512, sub=256)
        j = i // 2
        if i % 2 == 0:
            h = rmsnorm(x, norm_ffn[i], BF16)
            x = _dense_ffn(h, x, dense_w_gate[j:j + 1], dense_w_up[j:j + 1], dense_w_down[j:j + 1])
        else:
            h, route = rmsnorm_router(x, norm_ffn[i], moe_router[j])
            x = _moe_ffn(h, x, route, moe_w_gate[j], moe_w_up[j], moe_w_down[j])
    y = rmsnorm(x, norm_final, F32)
    return (y[:mp].reshape(bp, lp, d), y[mp:].reshape(bs, ls, d),
            jnp.stack(new_p[0]), jnp.stack(new_p[1]), jnp.stack(new_p[2]), jnp.stack(new_p[3]),
            dec_stacks[0].reshape(DEPTH, bs, SSM_HEADS, SSM_HEAD_DIM, D_STATE), jnp.stack(new_s[1]), dec_stacks[1], jnp.stack(new_s[3]))
```
